```python
import jax, jax.numpy as jnp
from jax import lax
import numpy as np

D_MODEL = 1024
BATCH = 32
SEQ = 256
DEPTH = 4
DEC_BATCH = 8
DEC_SEQ = 1024
PAST_LEN = 256

GRID_W = 64
H_MLA = 8
D_NOPE = 64
D_ROPE = 32
D_V = 64
KV_RANK = 256
ROPE_BASE = 10000.0
MLA_SCALE = (D_NOPE + D_ROPE) ** -0.5
Q_BLOCK = 128
H_GLA = 4
GLA_DK = 128
GLA_DV = 128
GLA_GATE_RANK = 16
GLA_TAU = 16.0
GLA_CHUNK = 64
FFT_GROUPS = 4
FFT_GROUP_W = 128
FFT_WIDTH = FFT_GROUPS * FFT_GROUP_W
N_EXPERTS = 16
D_EXPERT = 2048
CAPACITY_FACTOR = 2
N_BRANCHES = 3
N_MOD = 6
EPS = 1e-6
IN_SIZES = (H_MLA * (D_NOPE + D_ROPE), KV_RANK, D_ROPE, H_GLA * GLA_DK, H_GLA * GLA_DK, H_GLA * GLA_DV, H_GLA * GLA_DV, 2 * GLA_GATE_RANK, FFT_WIDTH, N_BRANCHES * D_MODEL)
IN_COLS = sum(IN_SIZES)

kernel_name = 'hybrid_dit_mla_gla_fnet_ec_step'


def rmsnorm(x, g):
    xf = x.astype(jnp.float32)
    y = xf * lax.rsqrt(jnp.mean(xf * xf, axis=-1, keepdims=True) + EPS)
    return y.astype(x.dtype) * g


def axial_rope_tables(n_tokens):
    rows = n_tokens // GRID_W
    row = jnp.repeat(jnp.arange(rows), GRID_W).astype(jnp.float32)
    col = jnp.tile(jnp.arange(GRID_W), rows).astype(jnp.float32)
    half = D_ROPE // 2
    inv = 1.0 / (ROPE_BASE ** (jnp.arange(0, half, 2, dtype=jnp.float32) / half))
    ang_r = row[:, None] * inv
    ang_c = col[:, None] * inv
    return (jnp.cos(ang_r), jnp.sin(ang_r), jnp.cos(ang_c), jnp.sin(ang_c))


def _rotate(x, cos, sin):
    n = x.shape[-1] // 2
    x1, x2 = x[..., :n], x[..., n:]
    shape = (cos.shape[0],) + (1,) * (x.ndim - 3) + (n,)
    cos = cos.reshape(shape).astype(x.dtype)
    sin = sin.reshape(shape).astype(x.dtype)
    return jnp.concatenate([x1 * cos - x2 * sin, x1 * sin + x2 * cos], axis=-1)


def axial_rope(x, tables):
    cr, sr, cc, sc = tables
    half = D_ROPE // 2
    return jnp.concatenate([_rotate(x[..., :half], cr, sr), _rotate(x[..., half:], cc, sc)], axis=-1)


def mla_attention(q_nope, q_rope, k_nope, k_rope, v):
    B, Tq, H, _ = q_nope.shape
    nb = Tq // Q_BLOCK

    def to_blocks(a):
        return jnp.moveaxis(a.reshape(B, nb, Q_BLOCK, H, a.shape[-1]), 1, 0)

    def block(args):
        qn, qr = args
        s = jnp.einsum('bqhd,bkhd->bhqk', qn, k_nope) + jnp.einsum('bqhd,bkd->bhqk', qr, k_rope)
        p = jax.nn.softmax(s.astype(jnp.float32) * MLA_SCALE, axis=-1).astype(v.dtype)
        return jnp.einsum('bhqk,bkhd->bqhd', p, v)

    o = lax.map(block, (to_blocks(q_nope), to_blocks(q_rope)))
    return jnp.moveaxis(o, 0, 1).reshape(B, Tq, H, v.shape[-1])


def gla_chunked(q, k, v, log_a, s0):
    B, T, H, dk = q.shape
    n = T // GLA_CHUNK

    def chunks(a):
        return a.reshape(B, n, GLA_CHUNK, H, a.shape[-1])

    q, k, v, la = chunks(q), chunks(k), chunks(v), chunks(log_a)
    b = jnp.cumsum(la, axis=2)
    b_last = b[:, :, -1:]
    q_t = q * jnp.exp(b)
    k_t = k * jnp.exp(-b)
    k_end = k * jnp.exp(b_last - b)
    mask = jnp.tril(jnp.ones((GLA_CHUNK, GLA_CHUNK), dtype=bool))
    att = jnp.where(mask, jnp.einsum('bnchd,bnshd->bnhcs', q_t, k_t), 0.0)
    o_intra = jnp.einsum('bnhcs,bnshv->bnchv', att, v)
    d_state = jnp.einsum('bnchd,bnchv->bnhdv', k_end, v)
    decay = jnp.exp(b_last[:, :, 0])

    def step(s, inp):
        dec, ds = inp
        return dec[..., None] * s + ds, s

    s_final, s_prev = lax.scan(step, s0, (jnp.moveaxis(decay, 1, 0), jnp.moveaxis(d_state, 1, 0)))
    s_prev = jnp.moveaxis(s_prev, 0, 1)
    o_inter = jnp.einsum('bnchd,bnhdv->bnchv', q_t, s_prev)
    return (o_intra + o_inter).reshape(B, T, H, v.shape[-1]), s_final


def gla_bidirectional(q, k, v, la_f, la_b, s0_f, s0_b):
    o_f, s_f = gla_chunked(q, k, v, la_f, s0_f)
    flip = lambda a: jnp.flip(a, axis=1)
    o_b, s_b = gla_chunked(flip(q), flip(k), flip(v), flip(la_b), s0_b)
    return o_f + flip(o_b), s_f, s_b


def fourier_mix(u):
    B, T, W = u.shape
    ug = u.reshape(B, T, FFT_GROUPS, FFT_GROUP_W).astype(jnp.float32)
    f = jnp.fft.fftn(ug, axes=(1, 3), norm='ortho').real
    return f.reshape(B, T, W).astype(u.dtype)


def expert_choice_ffn(h, router_w, w1, w3, w2):
    B, T, D = h.shape
    cap = CAPACITY_FACTOR * T // N_EXPERTS
    aff = jax.nn.softmax((h @ router_w).astype(jnp.float32), axis=-1)
    g, idx = lax.top_k(jnp.swapaxes(aff, 1, 2), cap)
    xg = jax.vmap(lambda hb, ib: hb[ib])(h, idx)
    a = jnp.einsum('becd,edf->becf', xg, w1)
    gt = jnp.einsum('becd,edf->becf', xg, w3)
    y = jnp.einsum('becf,efd->becd', jax.nn.silu(a) * gt, w2) * g[..., None].astype(h.dtype)
    return jax.vmap(lambda yb, ib: jnp.zeros((T, D), yb.dtype).at[ib.reshape(-1)].add(yb.reshape(-1, D)))(y, idx)


def trunk_layer(x, mod, n1, n2, w_in, kv_g, w_ukv, gate_w, gate_b, gla_g, w_mla_o, w_gla_o, w_fft_o, w_o,
                router_w, w1, w3, w2, rope=None, ctx=None):
    B, T, _ = x.shape
    f32 = jnp.float32
    shift1, scale1, gate1, shift2, scale2, gate2 = jnp.split(mod, N_MOD, axis=-1)
    h = rmsnorm(x, n1) * (1 + scale1) + shift1
    splits = np.cumsum(IN_SIZES)[:-1].tolist()
    q, ckv, kr, gq, gk, gv, gr, glr, uf, mg = jnp.split(h @ w_in, splits, axis=-1)

    q = q.reshape(B, T, H_MLA, D_NOPE + D_ROPE)
    q_nope, q_rope = q[..., :D_NOPE], q[..., D_NOPE:]
    ckv = rmsnorm(ckv, kv_g)
    if ctx is None:
        ckv_keys, kr_keys = ckv, kr
        s0_f = s0_b = jnp.zeros((B, H_GLA, GLA_DK, GLA_DV), f32)
    else:
        c_ckv, c_kr, s0_f, s0_b = ctx
        q_rope = axial_rope(q_rope, rope)
        ckv_keys = jnp.concatenate([c_ckv.astype(ckv.dtype), ckv], axis=1)
        kr_keys = jnp.concatenate([c_kr.astype(kr.dtype), axial_rope(kr, rope)], axis=1)
    kv = (ckv_keys @ w_ukv).reshape(B, -1, H_MLA, D_NOPE + D_V)
    o_mla = mla_attention(q_nope, q_rope, kv[..., :D_NOPE], kr_keys, kv[..., D_NOPE:])
    y_mla = o_mla.reshape(B, T, H_MLA * D_V) @ w_mla_o

    gq = gq.reshape(B, T, H_GLA, GLA_DK).astype(f32) * (GLA_DK ** -0.5)
    gk = gk.reshape(B, T, H_GLA, GLA_DK).astype(f32)
    gv = gv.reshape(B, T, H_GLA, GLA_DV).astype(f32)
    pre = jnp.einsum('btzr,zrk->btzk', glr.reshape(B, T, 2, GLA_GATE_RANK), gate_w) + gate_b
    la = (jax.nn.log_sigmoid(pre.astype(f32)) / GLA_TAU).reshape(B, T, 2, H_GLA, GLA_DK)
    o_gla, s_f, s_b = gla_bidirectional(gq, gk, gv, la[:, :, 0], la[:, :, 1], s0_f.astype(f32), s0_b.astype(f32))
    o_gla = rmsnorm(o_gla.astype(x.dtype), gla_g).reshape(B, T, H_GLA * GLA_DV) * jax.nn.silu(gr)
    y_gla = o_gla @ w_gla_o

    y_fft = fourier_mix(uf) @ w_fft_o

    g_mla, g_gla, g_fft = jnp.split(jax.nn.sigmoid(mg), N_BRANCHES, axis=-1)
    x = x + gate1 * ((g_mla * y_mla + g_gla * y_gla + g_fft * y_fft) @ w_o)

    h2 = rmsnorm(x, n2) * (1 + scale2) + shift2
    x = x + gate2 * expert_choice_ffn(h2, router_w, w1, w3, w2)
    if ctx is None:
        return x, (ckv, kr, jnp.stack([s_f, s_b], axis=1).astype(x.dtype))
    return x


def setup_inputs(seed: int = 0) -> dict:
    key = jax.random.key(seed)
    ks = jax.random.split(key, 32)
    D = D_MODEL

    def nrm(k, shape, scale):
        return jax.random.normal(k, shape, jnp.float32) * scale

    return {
        'x_prompt': nrm(ks[0], (BATCH, SEQ, D), 1.0),
        'x_sample': nrm(ks[1], (DEC_BATCH, DEC_SEQ, D), 1.0),
        'cache_ckv': nrm(ks[2], (DEC_BATCH, DEPTH, PAST_LEN, KV_RANK), 1.0),
        'cache_krope': nrm(ks[3], (DEC_BATCH, DEPTH, PAST_LEN, D_ROPE), 1.0),
        'state_gla': nrm(ks[4], (DEC_BATCH, DEPTH, 2, H_GLA, GLA_DK, GLA_DV), 1.0),
        'c': nrm(ks[5], (DEC_BATCH, D), 1.0),
        'c_ctx': nrm(ks[6], (D,), 1.0),
        'ada_w': nrm(ks[7], (DEPTH, D, N_MOD * D), 0.5 * D ** -0.5),
        'ada_b': nrm(ks[8], (DEPTH, N_MOD * D), 0.01),
        'norm1_g': 1.0 + nrm(ks[9], (DEPTH, D), 0.02),
        'norm2_g': 1.0 + nrm(ks[10], (DEPTH, D), 0.02),
        'w_in': nrm(ks[11], (DEPTH, D, IN_COLS), D ** -0.5),
        'mla_kv_norm_g': 1.0 + nrm(ks[12], (DEPTH, KV_RANK), 0.02),
        'mla_w_ukv': nrm(ks[13], (DEPTH, KV_RANK, H_MLA * (D_NOPE + D_V)), KV_RANK ** -0.5),
        'gla_gate_w': nrm(ks[14], (DEPTH, 2, GLA_GATE_RANK, H_GLA * GLA_DK), GLA_GATE_RANK ** -0.5),
        'gla_gate_b': nrm(ks[15], (DEPTH, 2, H_GLA * GLA_DK), 0.1),
        'gla_norm_g': 1.0 + nrm(ks[16], (DEPTH, GLA_DV), 0.02),
        'w_mla_out': nrm(ks[17], (DEPTH, H_MLA * D_V, D), (H_MLA * D_V) ** -0.5),
        'w_gla_out': nrm(ks[18], (DEPTH, H_GLA * GLA_DV, D), (H_GLA * GLA_DV) ** -0.5),
        'w_fft_out': nrm(ks[19], (DEPTH, FFT_WIDTH, D), FFT_WIDTH ** -0.5),
        'w_o': nrm(ks[20], (DEPTH, D, D), D ** -0.5),
        'router_w': nrm(ks[21], (DEPTH, D, N_EXPERTS), D ** -0.5),
        'exp_w1': nrm(ks[22], (DEPTH, N_EXPERTS, D, D_EXPERT), D ** -0.5),
        'exp_w3': nrm(ks[23], (DEPTH, N_EXPERTS, D, D_EXPERT), D ** -0.5),
        'exp_w2': nrm(ks[24], (DEPTH, N_EXPERTS, D_EXPERT, D), D_EXPERT ** -0.5),
        'final_norm_g': 1.0 + nrm(ks[25], (D,), 0.02),
    }


def reference(x_prompt, x_sample, cache_ckv, cache_krope, state_gla, c, c_ctx, ada_w, ada_b, norm1_g, norm2_g,
              w_in, mla_kv_norm_g, mla_w_ukv, gla_gate_w, gla_gate_b, gla_norm_g, w_mla_out, w_gla_out,
              w_fft_out, w_o, router_w, exp_w1, exp_w3, exp_w2, final_norm_g):
    rope = axial_rope_tables(x_sample.shape[1])
    silu_ctx = jax.nn.silu(c_ctx)
    silu_c = jax.nn.silu(c)
    xp, xs = x_prompt, x_sample
    ckv_list, kr_list, st_list = [], [], []
    for l in range(DEPTH):
        w = (norm1_g[l], norm2_g[l], w_in[l], mla_kv_norm_g[l], mla_w_ukv[l], gla_gate_w[l], gla_gate_b[l],
             gla_norm_g[l], w_mla_out[l], w_gla_out[l], w_fft_out[l], w_o[l], router_w[l],
             exp_w1[l], exp_w3[l], exp_w2[l])
        mod_p = (silu_ctx @ ada_w[l] + ada_b[l])[None, None, :]
        mod_s = (silu_c @ ada_w[l] + ada_b[l])[:, None, :]
        xp, (ckv_l, kr_l, st_l) = trunk_layer(xp, mod_p, *w)
        ckv_list.append(ckv_l)
        kr_list.append(kr_l)
        st_list.append(st_l)
        xs = trunk_layer(xs, mod_s, *w, rope=rope,
                         ctx=(cache_ckv[:, l], cache_krope[:, l], state_gla[:, l, 0], state_gla[:, l, 1]))
    y_prompt = rmsnorm(xp, final_norm_g)
    y_sample = rmsnorm(xs, final_norm_g)
    new_cache_ckv = jnp.stack(ckv_list, axis=1)
    new_cache_krope = jnp.stack(kr_list, axis=1)
    new_state_gla = jnp.stack(st_list, axis=1)
    return (y_prompt, y_sample, new_cache_ckv, new_cache_krope, new_state_gla)
```

```python
import functools

import jax
import jax.numpy as jnp
import numpy as np
from jax import lax
from jax.experimental import pallas as pl
from jax.experimental.pallas import tpu as pltpu

F32 = jnp.float32
BF16 = jnp.bfloat16

D_MODEL = 1024
DEPTH = 4
GRID_W = 64
H_MLA = 8
D_NOPE = 64
D_ROPE = 32
D_V = 64
KV_RANK = 256
ROPE_BASE = 10000.0
MLA_SCALE = (D_NOPE + D_ROPE) ** -0.5
H_GLA = 4
GLA_DK = 128
GLA_DV = 128
GLA_GATE_RANK = 16
GLA_TAU = 16.0
GLA_CHUNK = 64
FFT_GROUPS = 4
FFT_GROUP_W = 128
FFT_WIDTH = FFT_GROUPS * FFT_GROUP_W
N_EXPERTS = 16
D_EXPERT = 2048
CAPACITY_FACTOR = 2
N_MOD = 6
EPS = 1e-6

LANE = 128
ROW_TILE = 512
HEAD_PAD = 128
KR_OFF = D_NOPE
N_CHUNK = 7
MOD_ROWS = 16
EXPERT_F_TILE = 512
ROUTE_REFINE_STEPS = 16
VMEM_LIMIT = 56 * 1024 * 1024


def _cparams(sem):
    return pltpu.CompilerParams(dimension_semantics=sem, vmem_limit_bytes=VMEM_LIMIT)


def _dot(a, b):
    return jnp.dot(a, b, preferred_element_type=F32)


def _dot_nt(a, b):
    return lax.dot_general(a, b, (((1,), (1,)), ((), ())), preferred_element_type=F32)


def _dot_tn(a, b):
    return lax.dot_general(a, b, (((0,), (0,)), ((), ())), preferred_element_type=F32)


def _sigmoid(x):
    return 1.0 / (1.0 + jnp.exp(-x))


def _mods_kernel(c_ref, w_ref, b_ref, o_ref):
    c = c_ref[...]
    s = (c * _sigmoid(c)).astype(BF16)
    o_ref[...] = _dot(s, w_ref[...].astype(BF16)) + b_ref[...]


def _mods(c_all, ada_w, ada_b):
    D = D_MODEL
    return pl.pallas_call(
        _mods_kernel,
        grid=(DEPTH, N_MOD),
        in_specs=[
            pl.BlockSpec((MOD_ROWS, D), lambda l, k: (0, 0)),
            pl.BlockSpec((None, D, D), lambda l, k: (l, 0, k)),
            pl.BlockSpec((None, None, 1, D), lambda l, k: (l, k, 0, 0)),
        ],
        out_specs=pl.BlockSpec((None, None, MOD_ROWS, D), lambda l, k: (l, k, 0, 0)),
        out_shape=jax.ShapeDtypeStruct((DEPTH, N_MOD, MOD_ROWS, D), F32),
        compiler_params=_cparams(("arbitrary", "arbitrary")),
        name="adaln_mods",
    )(c_all, ada_w, ada_b.reshape(DEPTH, N_MOD, 1, D))


def _rms(x, g):
    return x * lax.rsqrt(jnp.mean(x * x, axis=-1, keepdims=True) + EPS) * g


def _inproj_kernel(x_ref, mod_ref, n1_ref, kvg_ref, rope_ref, w_ref, proj_ref, side_ref):
    x = x_ref[...]
    h = _rms(x, n1_ref[...]) * (1.0 + mod_ref[1]) + mod_ref[0]
    hb = h.astype(BF16)
    cos = rope_ref[0]
    sin_up = rope_ref[1]
    sin_dn = rope_ref[2]

    def rope(a):
        return a * cos + pltpu.roll(a, LANE - 8, 1) * sin_up + pltpu.roll(a, 8, 1) * sin_dn

    acc = _dot(hb, w_ref[0])
    for hd in range(H_MLA):
        sl = slice(hd * HEAD_PAD, (hd + 1) * HEAD_PAD)
        proj_ref[0, :, sl] = rope(acc[:, sl]).astype(BF16)

    acc = _dot(hb, w_ref[1])
    side_ref[:, :KV_RANK] = _rms(acc[:, :KV_RANK], kvg_ref[...])
    side_ref[:, KV_RANK:] = rope(acc[:, KV_RANK:KV_RANK + HEAD_PAD])
    proj_ref[1] = acc.astype(BF16)

    for c in (2, 3):
        proj_ref[c] = _dot(hb, w_ref[c]).astype(BF16)
    for c in (4, 5, 6):
        proj_ref[c] = _sigmoid(_dot(hb, w_ref[c])).astype(BF16)


def _inproj(x, mods, n1, kvg, rope_tab, w_in_r, l, nblk_ctx, blk_per_lat):
    R, D = x.shape
    nblk = R // ROW_TILE

    def mrow(i):
        return jnp.where(i < nblk_ctx, 0, 1 + (i - nblk_ctx) // blk_per_lat)

    def rrow(i):
        return jnp.where(i < nblk_ctx, 0, 1 + (i - nblk_ctx) % blk_per_lat)

    return pl.pallas_call(
        _inproj_kernel,
        grid=(nblk,),
        in_specs=[
            pl.BlockSpec((ROW_TILE, D), lambda i: (i, 0)),
            pl.BlockSpec((None, N_MOD, None, 1, D), lambda i: (l, 0, mrow(i), 0, 0)),
            pl.BlockSpec((None, 1, D), lambda i: (l, 0, 0)),
            pl.BlockSpec((None, 1, KV_RANK), lambda i: (l, 0, 0)),
            pl.BlockSpec((None, 3, ROW_TILE, HEAD_PAD), lambda i: (rrow(i), 0, 0, 0)),
            pl.BlockSpec((None, N_CHUNK, D, D), lambda i: (l, 0, 0, 0), pipeline_mode=pl.Buffered(1)),
        ],
        out_specs=[
            pl.BlockSpec((N_CHUNK, ROW_TILE, D), lambda i: (0, i, 0)),
            pl.BlockSpec((ROW_TILE, KV_RANK + HEAD_PAD), lambda i: (i, 0)),
        ],
        out_shape=[
            jax.ShapeDtypeStruct((N_CHUNK, R, D), BF16),
            jax.ShapeDtypeStruct((R, KV_RANK + HEAD_PAD), F32),
        ],
        compiler_params=_cparams(("arbitrary",)),
        name="inproj",
    )(x, mods, n1, kvg, rope_tab, w_in_r)


def _attn_kernel(*refs, t_q, has_cache, q_blk):
    if has_cache:
        q_ref, side_ref, cckv_ref, ckr_ref, wuk_ref, wuv_ref, _, o_ref = refs
        ckv = jnp.concatenate([cckv_ref[...], side_ref[:, :KV_RANK]], axis=0)
        krp = jnp.concatenate([ckr_ref[...], side_ref[:, KV_RANK:]], axis=0)
    else:
        q_ref, side_ref, wuk_ref, wuv_ref, o_ref = refs
        ckv = side_ref[:, :KV_RANK]
        krp = side_ref[:, KV_RANK:]
    ckvb = ckv.astype(BF16)
    k_all = _dot(ckvb, wuk_ref[...])
    v_all = _dot(ckvb, wuv_ref[...]).astype(BF16)
    for pair in range(H_MLA // 2):
        for qb in range(t_q // q_blk):
            rows = slice(qb * q_blk, (qb + 1) * q_blk)
            o_pair = None
            for hd in (2 * pair, 2 * pair + 1):
                sl = slice(hd * HEAD_PAD, (hd + 1) * HEAD_PAD)
                k_h = (k_all[:, sl] + krp).astype(BF16)
                s = _dot_nt(q_ref[rows, sl], k_h) * MLA_SCALE
                m = jnp.max(s, axis=-1, keepdims=True)
                p = jnp.exp(s - m)
                den = jnp.sum(p, axis=-1, keepdims=True)
                o_h = _dot(p.astype(BF16), v_all[:, sl]) / den
                o_pair = o_h if o_pair is None else o_pair + o_h
            o_ref[rows, pair * LANE:(pair + 1) * LANE] = o_pair.astype(BF16)


def _attn(proj, side, wuk, wuv, l, *, nseq, t_q, row_blk0, cache=None, prev=None):
    R = side.shape[0]
    assert (cache is None) == (prev is None)
    in_specs = [
        pl.BlockSpec((None, t_q, D_MODEL), lambda b: (0, row_blk0 + b, 0)),
        pl.BlockSpec((t_q, KV_RANK + HEAD_PAD), lambda b: (row_blk0 + b, 0)),
    ]
    args = [proj, side]
    if cache is not None:
        cckv, ckr = cache
        past = cckv.shape[2]
        in_specs += [
            pl.BlockSpec((None, None, past, KV_RANK), lambda b: (b, l, 0, 0)),
            pl.BlockSpec((None, None, past, HEAD_PAD), lambda b: (b, l, 0, 0)),
        ]
        args += [cckv, ckr]
    in_specs += [
        pl.BlockSpec((None, KV_RANK, H_MLA * HEAD_PAD), lambda b: (l, 0, 0)),
        pl.BlockSpec((None, KV_RANK, H_MLA * HEAD_PAD), lambda b: (l, 0, 0)),
    ]
    args += [wuk, wuv]
    aliases = {}
    if prev is not None:
        in_specs.append(pl.BlockSpec(memory_space=pl.ANY))
        args.append(prev)
        aliases = {len(args) - 1: 0}
    return pl.pallas_call(
        functools.partial(_attn_kernel, t_q=t_q, has_cache=cache is not None, q_blk=min(t_q, 256)),
        grid=(nseq,),
        in_specs=in_specs,
        out_specs=pl.BlockSpec((t_q, H_MLA * D_V), lambda b: (row_blk0 + b, 0)),
        out_shape=jax.ShapeDtypeStruct((R, H_MLA * D_V), BF16),
        input_output_aliases=aliases,
        compiler_params=_cparams(("arbitrary",)),
        name="mla_attn",
    )(*args)


def _gla_kernel(*refs, t, has_state, want_state):
    refs = list(refs)
    q_ref, k_ref, v_ref, r_ref, glr_ref, wg_ref, bg_ref, g_ref = refs[:8]
    pos = 8
    s0_ref = None
    if has_state:
        s0_ref = refs[pos]
        pos += 2
    o_ref = refs[pos]
    pos += 1
    so_ref = None
    if want_state:
        so_ref = refs[pos]
        pos += 1
    b_ref, of_ref, ob_ref, st_ref = refs[pos:pos + 4]

    CH = GLA_CHUNK
    BLK = 256
    HW = H_GLA * GLA_DK
    n_chunks = t // CH

    pre = _dot(glr_ref[...], wg_ref[...]) + bg_ref[...]
    la = (jnp.minimum(pre, 0.0) - jnp.log(1.0 + jnp.exp(-jnp.abs(pre)))) * (1.0 / GLA_TAU)
    la_hi = la.astype(BF16)
    la_lo = (la - la_hi.astype(F32)).astype(BF16)
    ri = lax.broadcasted_iota(jnp.int32, (BLK, BLK), 0)
    ci = lax.broadcasted_iota(jnp.int32, (BLK, BLK), 1)
    same = (ri // CH) == (ci // CH)
    tri_f = jnp.where(same & (ci <= ri), 1.0, 0.0).astype(BF16)
    tri_b = jnp.where(same & (ci >= ri), 1.0, 0.0).astype(BF16)
    for blk in range(t // BLK):
        rows = slice(blk * BLK, (blk + 1) * BLK)
        hi, lo = la_hi[rows], la_lo[rows]
        b_ref[rows, :HW] = _dot(tri_f, hi[:, :HW]) + _dot(tri_f, lo[:, :HW])
        b_ref[rows, HW:] = _dot(tri_b, hi[:, HW:]) + _dot(tri_b, lo[:, HW:])

    for h in range(H_GLA):
        for d in range(2):
            if has_state:
                st_ref[2 * h + d] = s0_ref[d, h].T
            else:
                st_ref[2 * h + d] = jnp.zeros((GLA_DV, GLA_DK), F32)

    rc = lax.broadcasted_iota(jnp.int32, (CH, CH), 0)
    cc = lax.broadcasted_iota(jnp.int32, (CH, CH), 1)
    masks = (cc <= rc, cc >= rc)
    scale = GLA_DK ** -0.5

    def chain(h, d, c):
        if isinstance(c, int):
            rows = slice(c * CH, (c + 1) * CH)
        else:
            rows = pl.ds(pl.multiple_of(c * CH, CH), CH)
        cols = slice(h * GLA_DK, (h + 1) * GLA_DK)
        b = b_ref[rows, d * HW + h * GLA_DK:d * HW + (h + 1) * GLA_DK]
        b_last = b[CH - 1:CH] if d == 0 else b[0:1]
        q = q_ref[rows, cols].astype(F32) * scale
        k = k_ref[rows, cols].astype(F32)
        v = v_ref[rows, cols]
        q_t = (q * jnp.exp(b)).astype(BF16)
        k_t = (k * jnp.exp(-b)).astype(BF16)
        k_end = (k * jnp.exp(b_last - b)).astype(BF16)
        st = st_ref[2 * h + d]
        att = jnp.where(masks[d], _dot_nt(q_t, k_t), 0.0).astype(BF16)
        o = _dot(att, v) + _dot_nt(q_t, st.astype(BF16))
        if d == 0:
            of_ref[rows, cols] = o
        else:
            ob_ref[rows, cols] = o
        st_ref[2 * h + d] = st * jnp.exp(b_last) + _dot_tn(v, k_end)

    def step(n):
        for h in range(H_GLA):
            chain(h, 0, n)
            chain(h, 1, n_chunks - 1 - n)

    if n_chunks <= 4:
        for n in range(n_chunks):
            step(n)
    else:
        def body(n, carry):
            step(n)
            return carry
        lax.fori_loop(0, n_chunks, body, 0, unroll=2)

    if want_state:
        for h in range(H_GLA):
            for d in range(2):
                so_ref[d, h] = st_ref[2 * h + d].T

    g = g_ref[...]
    for h in range(H_GLA):
        cols = slice(h * GLA_DV, (h + 1) * GLA_DV)
        o = _rms(of_ref[:, cols] + ob_ref[:, cols], g)
        r = r_ref[:, cols].astype(F32)
        o_ref[:, cols] = (o * (r * _sigmoid(r))).astype(BF16)


def _gla(proj, glr_w, glr_b, gla_g, l, *, nseq, t, row_blk0, state=None, prev=None, want_state=False):
    R = proj.shape[1]
    HW = H_GLA * GLA_DK
    assert (state is None) == (prev is None)

    def spec(chunk, half):
        return pl.BlockSpec((None, t, HW), lambda b: (chunk, row_blk0 + b, half))

    in_specs = [
        spec(2, 0), spec(2, 1), spec(3, 0), spec(3, 1),
        pl.BlockSpec((None, t, LANE), lambda b: (1, row_blk0 + b, 3)),
        pl.BlockSpec((None, LANE, 2 * HW), lambda b: (l, 0, 0)),
        pl.BlockSpec((None, 1, 2 * HW), lambda b: (l, 0, 0)),
        pl.BlockSpec((None, 1, GLA_DV), lambda b: (l, 0, 0)),
    ]
    args = [proj, proj, proj, proj, proj, glr_w, glr_b, gla_g]
    aliases = {}
    if state is not None:
        in_specs.append(pl.BlockSpec((None, None, 2, H_GLA, GLA_DK, GLA_DV), lambda b: (b, l, 0, 0, 0, 0)))
        in_specs.append(pl.BlockSpec(memory_space=pl.ANY))
        args += [state, prev]
        aliases = {len(args) - 1: 0}
    out_specs = [pl.BlockSpec((t, HW), lambda b: (row_blk0 + b, 0))]
    out_shape = [jax.ShapeDtypeStruct((R, HW), BF16)]
    if want_state:
        out_specs.append(pl.BlockSpec((None, 2, H_GLA, GLA_DK, GLA_DV), lambda b: (b, 0, 0, 0, 0)))
        out_shape.append(jax.ShapeDtypeStruct((nseq, 2, H_GLA, GLA_DK, GLA_DV), F32))
    return pl.pallas_call(
        functools.partial(_gla_kernel, t=t, has_state=state is not None, want_state=want_state),
        grid=(nseq,),
        in_specs=in_specs,
        out_specs=out_specs,
        out_shape=out_shape,
        scratch_shapes=[pltpu.VMEM((t, 2 * HW), F32), pltpu.VMEM((t, HW), F32), pltpu.VMEM((t, HW), F32),
                        pltpu.VMEM((2 * H_GLA, GLA_DV, GLA_DK), F32)],
        input_output_aliases=aliases,
        compiler_params=_cparams(("arbitrary",)),
        name="gla",
    )(*args)


def _fft_kernel(u_ref, cw_ref, sw_ref, ct_ref, *rest):
    o_ref = rest[-1]
    u = u_ref[...]
    a = _dot(u, cw_ref[...]).astype(BF16)
    b = _dot(u, sw_ref[...]).astype(BF16)
    ab = jnp.concatenate([a, b], axis=0)
    o_ref[...] = _dot(ct_ref[...], ab).astype(BF16)


def _fft(proj, cw, sw, ct, *, nseq, t, row_blk0, prev=None):
    R = proj.shape[1]
    W = FFT_WIDTH
    in_specs = [
        pl.BlockSpec((None, t, W), lambda b: (1, row_blk0 + b, 1)),
        pl.BlockSpec((W, W), lambda b: (0, 0)),
        pl.BlockSpec((W, W), lambda b: (0, 0)),
        pl.BlockSpec((t, 2 * t), lambda b: (0, 0)),
    ]
    args = [proj, cw, sw, ct]
    aliases = {}
    if prev is not None:
        in_specs.append(pl.BlockSpec(memory_space=pl.ANY))
        args.append(prev)
        aliases = {len(args) - 1: 0}
    return pl.pallas_call(
        _fft_kernel,
        grid=(nseq,),
        in_specs=in_specs,
        out_specs=pl.BlockSpec((t, W), lambda b: (row_blk0 + b, 0)),
        out_shape=jax.ShapeDtypeStruct((R, W), BF16),
        input_output_aliases=aliases,
        compiler_params=_cparams(("arbitrary",)),
        name="fourier_mix",
    )(*args)


def _mix_kernel(x_ref, om_ref, og_ref, of_ref, g0_ref, g1_ref, g2_ref, mod_ref, n2_ref,
                wm_ref, wg_ref, wf_ref, wo_ref, rw_ref, x1_ref, h2_ref, aff_ref):
    ym = _dot(om_ref[...], wm_ref[...])
    yg = _dot(og_ref[...], wg_ref[...])
    yf = _dot(of_ref[...], wf_ref[...])
    merged = g0_ref[...].astype(F32) * ym + g1_ref[...].astype(F32) * yg + g2_ref[...].astype(F32) * yf
    z = _dot(merged.astype(BF16), wo_ref[...])
    x1 = x_ref[...] + mod_ref[2] * z
    x1_ref[...] = x1
    h2 = _rms(x1, n2_ref[...]) * (1.0 + mod_ref[4]) + mod_ref[3]
    h2_ref[...] = h2.astype(BF16)
    h_hi = h2.astype(BF16)
    h_lo = (h2 - h_hi.astype(F32)).astype(BF16)
    lg = _dot(h_hi, rw_ref[...])
    lg = lg[:, :LANE] + lg[:, LANE:] + _dot(h_lo, rw_ref[:, :LANE])
    logits = lg.T[:N_EXPERTS]
    m = jnp.max(logits, axis=0, keepdims=True)
    p = jnp.exp(logits - m)
    aff_ref[...] = p / jnp.sum(p, axis=0, keepdims=True)


def _mix(x, o_mla, o_gla, o_fft, proj, mods, n2, wm, wg, wf, wo, rwt, l, nblk_ctx, blk_per_lat):
    R, D = x.shape
    nblk = R // ROW_TILE
    HW = H_MLA * D_V

    def mrow(i):
        return jnp.where(i < nblk_ctx, 0, 1 + (i - nblk_ctx) // blk_per_lat)

    def wspec(k):
        return pl.BlockSpec((None, k, D), lambda i: (l, 0, 0))

    return pl.pallas_call(
        _mix_kernel,
        grid=(nblk,),
        in_specs=[
            pl.BlockSpec((ROW_TILE, D), lambda i: (i, 0)),
            pl.BlockSpec((ROW_TILE, HW), lambda i: (i, 0)),
            pl.BlockSpec((ROW_TILE, HW), lambda i: (i, 0)),
            pl.BlockSpec((ROW_TILE, HW), lambda i: (i, 0)),
            pl.BlockSpec((None, ROW_TILE, D), lambda i: (4, i, 0)),
            pl.BlockSpec((None, ROW_TILE, D), lambda i: (5, i, 0)),
            pl.BlockSpec((None, ROW_TILE, D), lambda i: (6, i, 0)),
            pl.BlockSpec((None, N_MOD, None, 1, D), lambda i: (l, 0, mrow(i), 0, 0)),
            pl.BlockSpec((None, 1, D), lambda i: (l, 0, 0)),
            wspec(HW), wspec(HW), wspec(HW), wspec(D),
            pl.BlockSpec((None, D, 2 * LANE), lambda i: (l, 0, 0)),
        ],
        out_specs=[
            pl.BlockSpec((ROW_TILE, D), lambda i: (i, 0)),
            pl.BlockSpec((ROW_TILE, D), lambda i: (i, 0)),
            pl.BlockSpec((N_EXPERTS, ROW_TILE), lambda i: (0, i)),
        ],
        out_shape=[
            jax.ShapeDtypeStruct((R, D), F32),
            jax.ShapeDtypeStruct((R, D), BF16),
            jax.ShapeDtypeStruct((N_EXPERTS, R), F32),
        ],
        compiler_params=_cparams(("arbitrary",)),
        name="mix_out",
    )(x, o_mla, o_gla, o_fft, proj, proj, proj, mods, n2, wm, wg, wf, wo, rwt)


def _route_kernel(a_ref, slot_ref, *, cap):
    a = a_ref[...]
    rows, t = a.shape

    def count_ge(x):
        return jnp.sum(jnp.where(a >= x, 1.0, 0.0), axis=1, keepdims=True)

    thr = jnp.zeros((rows, 1), jnp.int32)
    for bit in range(30, -1, -1):
        cand = thr | (1 << bit)
        thr = jnp.where(count_ge(pltpu.bitcast(cand, F32)) >= cap, cand, thr)
    lo = pltpu.bitcast(thr, F32)
    hi = pltpu.bitcast(thr + 1, F32)
    for _ in range(ROUTE_REFINE_STEPS):
        mid = lo + (hi - lo) * 0.5
        keep = count_ge(mid) >= cap
        lo = jnp.where(keep, mid, lo)
        hi = jnp.where(keep, hi, mid)
    gt = a > lo
    eq = a == lo
    need = cap - jnp.sum(jnp.where(gt, 1.0, 0.0), axis=1, keepdims=True)
    ri = lax.broadcasted_iota(jnp.int32, (t, t), 0)
    ci = lax.broadcasted_iota(jnp.int32, (t, t), 1)
    before = jnp.where(ri < ci, 1.0, 0.0).astype(BF16)
    eq_rank = _dot(jnp.where(eq, 1.0, 0.0).astype(BF16), before)
    sel = gt | (eq & (eq_rank < need))
    slot = _dot(jnp.where(sel, 1.0, 0.0).astype(BF16), before)
    slot_ref[...] = jnp.where(sel, slot, -1.0)


def _route(a, cap):
    return pl.pallas_call(
        functools.partial(_route_kernel, cap=cap),
        out_shape=jax.ShapeDtypeStruct(a.shape, F32),
        compiler_params=pltpu.CompilerParams(vmem_limit_bytes=VMEM_LIMIT),
        name="route",
    )(a)


def _gather_kernel(h_ref, slot_ref, aff_ref, xg_ref, g_ref, p_ref, *, cap):
    t = h_ref.shape[0]
    ci = lax.broadcasted_iota(jnp.int32, (cap, t), 0).astype(F32)
    for e in range(N_EXPERTS):
        hit = ci == slot_ref[e:e + 1, :]
        p_ref[e * cap:(e + 1) * cap, :] = jnp.where(hit, 1.0, 0.0).astype(BF16)
        g_ref[e] = jnp.sum(jnp.where(hit, aff_ref[e:e + 1, :], 0.0), axis=1, keepdims=True)
    xg = _dot(p_ref[...], h_ref[...])
    for e in range(N_EXPERTS):
        xg_ref[e] = xg[e * cap:(e + 1) * cap].astype(BF16)


def _gather(h2, slot, aff, xg_prev, g_prev, *, nseq, t, cap, row_blk0, slot_blk0, n_slots):
    D = D_MODEL
    in_specs = [
        pl.BlockSpec((t, D), lambda b: (row_blk0 + b, 0)),
        pl.BlockSpec((None, N_EXPERTS, t), lambda b: (b, 0, 0)),
        pl.BlockSpec((None, N_EXPERTS, t), lambda b: (b, 0, 0)),
    ]
    args = [h2, slot, aff]
    aliases = {}
    if xg_prev is not None:
        in_specs += [pl.BlockSpec(memory_space=pl.ANY), pl.BlockSpec(memory_space=pl.ANY)]
        args += [xg_prev, g_prev]
        aliases = {3: 0, 4: 1}

    def body(h_ref, slot_ref, aff_ref, *rest):
        _gather_kernel(h_ref, slot_ref, aff_ref, *rest[-3:], cap=cap)

    return pl.pallas_call(
        body,
        grid=(nseq,),
        in_specs=in_specs,
        out_specs=[
            pl.BlockSpec((N_EXPERTS, cap, D), lambda b: (0, slot_blk0 + b, 0)),
            pl.BlockSpec((N_EXPERTS, cap, 1), lambda b: (0, slot_blk0 + b, 0)),
        ],
        out_shape=[
            jax.ShapeDtypeStruct((N_EXPERTS, n_slots, D), BF16),
            jax.ShapeDtypeStruct((N_EXPERTS, n_slots, 1), F32),
        ],
        scratch_shapes=[pltpu.VMEM((N_EXPERTS * cap, t), BF16)],
        input_output_aliases=aliases,
        compiler_params=_cparams(("arbitrary",)),
        name="moe_gather",
    )(*args)


def _expert_kernel(x_ref, g_ref, w1_ref, w3_ref, w2_ref, y_ref, acc_ref, *, sub):
    f = pl.program_id(1)
    nf = pl.num_programs(1)
    w1 = w1_ref[...].astype(BF16)
    w3 = w3_ref[...].astype(BF16)
    w2 = w2_ref[...].astype(BF16)
    n = x_ref.shape[0]

    @pl.when(f == 0)
    def _():
        acc_ref[...] = jnp.zeros_like(acc_ref)

    for r in range(n // sub):
        rows = slice(r * sub, (r + 1) * sub)
        x = x_ref[rows, :]
        a = _dot(x, w1)
        gt = _dot(x, w3)
        hm = (a * _sigmoid(a) * gt).astype(BF16)
        acc_ref[rows, :] += _dot(hm, w2)

    @pl.when(f == nf - 1)
    def _():
        y_ref[...] = (acc_ref[...] * g_ref[...]).astype(BF16)


def _experts(xg, g, w1, w3, w2, l):
    E, n, D = xg.shape
    FT = EXPERT_F_TILE
    return pl.pallas_call(
        functools.partial(_expert_kernel, sub=min(n, 512)),
        grid=(E, D_EXPERT // FT),
        in_specs=[
            pl.BlockSpec((None, n, D), lambda e, f: (e, 0, 0)),
            pl.BlockSpec((None, n, 1), lambda e, f: (e, 0, 0)),
            pl.BlockSpec((None, None, D, FT), lambda e, f: (l, e, 0, f)),
            pl.BlockSpec((None, None, D, FT), lambda e, f: (l, e, 0, f)),
            pl.BlockSpec((None, None, FT, D), lambda e, f: (l, e, f, 0)),
        ],
        out_specs=pl.BlockSpec((None, n, D), lambda e, f: (e, 0, 0)),
        out_shape=jax.ShapeDtypeStruct((E, n, D), BF16),
        scratch_shapes=[pltpu.VMEM((n, D), F32)],
        compiler_params=_cparams(("arbitrary", "arbitrary")),
        name="moe_experts",
    )(xg, g, w1, w3, w2)


def _scatter_kernel(x_ref, y_ref, slot_ref, mod_ref, *rest, cap):
    o_ref = rest[-1]
    t = x_ref.shape[0]
    n = N_EXPERTS * cap
    er = lax.broadcasted_iota(jnp.int32, (N_EXPERTS, n), 0)
    ec = lax.broadcasted_iota(jnp.int32, (N_EXPERTS, n), 1)
    expand = jnp.where(ec // cap == er, 1.0, 0.0).astype(BF16)
    slot_wide = _dot(slot_ref[...].astype(BF16), expand)
    want = (lax.broadcasted_iota(jnp.int32, (t, n), 1) % cap).astype(F32)
    pt = jnp.where(slot_wide == want, 1.0, 0.0).astype(BF16)
    y = jnp.concatenate([y_ref[e] for e in range(N_EXPERTS)], axis=0)
    o_ref[...] = x_ref[...] + mod_ref[5] * _dot(pt, y)


def _scatter(x1, y, slot_t, mods, x_prev, l, *, nseq, t, cap, row_blk0, slot_blk0, mod_row0, mod_step):
    R, D = x1.shape
    in_specs = [
        pl.BlockSpec((t, D), lambda b: (row_blk0 + b, 0)),
        pl.BlockSpec((N_EXPERTS, cap, D), lambda b: (0, slot_blk0 + b, 0)),
        pl.BlockSpec((None, t, N_EXPERTS), lambda b: (b, 0, 0)),
        pl.BlockSpec((None, N_MOD, None, 1, D), lambda b: (l, 0, mod_row0 + mod_step * b, 0, 0)),
    ]
    args = [x1, y, slot_t, mods]
    aliases = {}
    if x_prev is not None:
        in_specs.append(pl.BlockSpec(memory_space=pl.ANY))
        args.append(x_prev)
        aliases = {4: 0}
    return pl.pallas_call(
        functools.partial(_scatter_kernel, cap=cap),
        grid=(nseq,),
        in_specs=in_specs,
        out_specs=pl.BlockSpec((t, D), lambda b: (row_blk0 + b, 0)),
        out_shape=jax.ShapeDtypeStruct((R, D), F32),
        input_output_aliases=aliases,
        compiler_params=_cparams(("arbitrary",)),
        name="moe_scatter",
    )(*args)


def _final_kernel(x_ref, g_ref, o_ref):
    o_ref[...] = _rms(x_ref[...], g_ref[...])


def _final_norm(x, g, *, row_blk0, nblk):
    D = D_MODEL
    return pl.pallas_call(
        _final_kernel,
        grid=(nblk,),
        in_specs=[
            pl.BlockSpec((ROW_TILE, D), lambda i: (row_blk0 + i, 0)),
            pl.BlockSpec((1, D), lambda i: (0, 0)),
        ],
        out_specs=pl.BlockSpec((ROW_TILE, D), lambda i: (i, 0)),
        out_shape=jax.ShapeDtypeStruct((nblk * ROW_TILE, D), F32),
        compiler_params=_cparams(("arbitrary",)),
        name="final_norm",
    )(x, g)


def _rope_tables(t_lat):
    half = D_ROPE // 2
    nfreq = half // 2
    pos = jnp.arange(t_lat)
    row = (pos // GRID_W).astype(F32)
    col = (pos % GRID_W).astype(F32)
    inv = 1.0 / (ROPE_BASE ** (jnp.arange(0, half, 2, dtype=F32) / half))
    cos = jnp.ones((t_lat, HEAD_PAD), F32)
    s_up = jnp.zeros((t_lat, HEAD_PAD), F32)
    s_dn = jnp.zeros((t_lat, HEAD_PAD), F32)
    for part, p in enumerate((row, col)):
        ang = p[:, None] * inv
        c, s = jnp.cos(ang), jnp.sin(ang)
        o = KR_OFF + part * half
        cos = cos.at[:, o:o + nfreq].set(c).at[:, o + nfreq:o + half].set(c)
        s_up = s_up.at[:, o:o + nfreq].set(-s)
        s_dn = s_dn.at[:, o + nfreq:o + half].set(s)
    lat = jnp.stack([cos, s_up, s_dn], axis=0).reshape(3, t_lat // ROW_TILE, ROW_TILE, HEAD_PAD)
    lat = jnp.moveaxis(lat, 1, 0)
    ident = jnp.stack([jnp.ones((ROW_TILE, HEAD_PAD), F32), jnp.zeros((ROW_TILE, HEAD_PAD), F32),
                       jnp.zeros((ROW_TILE, HEAD_PAD), F32)], axis=0)[None]
    return jnp.concatenate([ident, lat], axis=0)


def _dft_mats(n, scale):
    k = jnp.arange(n)
    ang = (2.0 * np.pi / n) * ((k[:, None] * k[None, :]) % n).astype(F32)
    return jnp.cos(ang) * scale, jnp.sin(ang) * scale


def _dft_tables(t):
    cw, sw = _dft_mats(FFT_GROUP_W, FFT_GROUP_W ** -0.5)
    eye = jnp.eye(FFT_GROUPS, dtype=F32)
    cw_bd = jnp.kron(eye, cw).astype(BF16)
    sw_bd = jnp.kron(eye, sw).astype(BF16)
    ct, st = _dft_mats(t, t ** -0.5)
    return cw_bd, sw_bd, jnp.concatenate([ct, -st], axis=1).astype(BF16)


def _layout_w_in(w_in):
    L, D, _ = w_in.shape
    sizes = (H_MLA * (D_NOPE + D_ROPE), KV_RANK, D_ROPE, H_GLA * GLA_DK, H_GLA * GLA_DK, H_GLA * GLA_DV,
             H_GLA * GLA_DV, 2 * GLA_GATE_RANK, FFT_WIDTH, 3 * D_MODEL)
    offs = np.concatenate([[0], np.cumsum(sizes)])
    seg = [w_in[:, :, offs[i]:offs[i + 1]] for i in range(len(sizes))]
    q, ckv, kr, gq, gk, gv, gr, glr, uf, mg = seg
    q = q.reshape(L, D, H_MLA, D_NOPE + D_ROPE)
    q = jnp.pad(q, ((0, 0), (0, 0), (0, 0), (0, HEAD_PAD - D_NOPE - D_ROPE))).reshape(L, D, H_MLA * HEAD_PAD)
    kr = jnp.pad(kr, ((0, 0), (0, 0), (KR_OFF, HEAD_PAD - KR_OFF - D_ROPE)))
    glr = jnp.pad(glr, ((0, 0), (0, 0), (0, LANE - 2 * GLA_GATE_RANK)))
    cols = jnp.concatenate([q, ckv, kr, glr, uf, gq, gk, gv, gr, mg], axis=-1).astype(BF16)
    return jnp.moveaxis(cols.reshape(L, D, N_CHUNK, D), 2, 1)


def _layout_w_ukv(w_ukv):
    L = w_ukv.shape[0]
    w = w_ukv.reshape(L, KV_RANK, H_MLA, D_NOPE + D_V)
    k = jnp.pad(w[..., :D_NOPE], ((0, 0), (0, 0), (0, 0), (0, HEAD_PAD - D_NOPE)))
    v = w[..., D_NOPE:].reshape(L, KV_RANK, H_MLA // 2, 2, D_V)
    z = jnp.zeros_like(v[:, :, :, 0])
    v_even = jnp.concatenate([v[:, :, :, 0], z], axis=-1)
    v_odd = jnp.concatenate([z, v[:, :, :, 1]], axis=-1)
    v2 = jnp.stack([v_even, v_odd], axis=3)
    return (k.reshape(L, KV_RANK, H_MLA * HEAD_PAD).astype(BF16),
            v2.reshape(L, KV_RANK, H_MLA * HEAD_PAD).astype(BF16))


def _layout_gate(gate_w, gate_b):
    L = gate_w.shape[0]
    HW = H_GLA * GLA_DK
    wg = jnp.zeros((L, LANE, 2 * HW), F32)
    for d in range(2):
        wg = wg.at[:, d * GLA_GATE_RANK:(d + 1) * GLA_GATE_RANK, d * HW:(d + 1) * HW].set(gate_w[:, d])
    return wg.astype(BF16), gate_b.reshape(L, 1, 2 * HW)


def kernel(x_prompt, x_sample, cache_ckv, cache_krope, state_gla, c, c_ctx, ada_w, ada_b, norm1_g, norm2_g, w_in, mla_kv_norm_g, mla_w_ukv, gla_gate_w, gla_gate_b, gla_norm_g, w_mla_out, w_gla_out, w_fft_out, w_o, router_w, exp_w1, exp_w3, exp_w2, final_norm_g):
    nb_c, t_c, D = x_prompt.shape
    nb_l, t_l, _ = x_sample.shape
    rows_c, rows_l = nb_c * t_c, nb_l * t_l
    assert rows_c % t_l == 0 and t_c % 256 == 0 and t_l % ROW_TILE == 0 and 1 + nb_l <= MOD_ROWS
    cap_c = CAPACITY_FACTOR * t_c // N_EXPERTS
    cap_l = CAPACITY_FACTOR * t_l // N_EXPERTS
    n_slots = nb_c * cap_c + nb_l * cap_l
    assert (nb_c * cap_c) % cap_l == 0
    nblk_ctx = rows_c // ROW_TILE
    blk_per_lat = t_l // ROW_TILE

    x = jnp.concatenate([x_prompt.reshape(rows_c, D), x_sample.reshape(rows_l, D)], axis=0)
    c_all = jnp.zeros((MOD_ROWS, D), F32).at[0].set(c_ctx).at[1:1 + nb_l].set(c)
    mods = _mods(c_all, ada_w, ada_b).reshape(DEPTH, N_MOD, MOD_ROWS, 1, D)
    rope_tab = _rope_tables(t_l)
    dft_c = _dft_tables(t_c)
    dft_l = _dft_tables(t_l)
    w_in_r = _layout_w_in(w_in)
    wuk, wuv = _layout_w_ukv(mla_w_ukv)
    wgate, bgate = _layout_gate(gla_gate_w, gla_gate_b)
    n1 = norm1_g.reshape(DEPTH, 1, D)
    n2 = norm2_g.reshape(DEPTH, 1, D)
    kvg = mla_kv_norm_g.reshape(DEPTH, 1, KV_RANK)
    glg = gla_norm_g.reshape(DEPTH, 1, GLA_DV)
    wm = w_mla_out.astype(BF16)
    wgo = w_gla_out.astype(BF16)
    wfo = w_fft_out.astype(BF16)
    wo = w_o.astype(BF16)
    rw_hi = router_w.astype(BF16)
    rw_lo = (router_w - rw_hi.astype(F32)).astype(BF16)
    rwt = jnp.zeros((DEPTH, D, 2 * LANE), BF16).at[:, :, :N_EXPERTS].set(rw_hi)
    rwt = rwt.at[:, :, LANE:LANE + N_EXPERTS].set(rw_lo)
    ckr_pad = jnp.pad(cache_krope, ((0, 0), (0, 0), (0, 0), (KR_OFF, HEAD_PAD - KR_OFF - D_ROPE)))

    ckv_out, kr_out, st_out = [], [], []
    for l in range(DEPTH):
        proj, side = _inproj(x, mods, n1, kvg, rope_tab, w_in_r, l, nblk_ctx, blk_per_lat)
        ckv_out.append(side[:rows_c, :KV_RANK].reshape(nb_c, t_c, KV_RANK))
        kr_out.append(side[:rows_c, KV_RANK + KR_OFF:KV_RANK + KR_OFF + D_ROPE].reshape(nb_c, t_c, D_ROPE))

        o_mla = _attn(proj, side, wuk, wuv, l, nseq=nb_c, t_q=t_c, row_blk0=0)
        o_mla = _attn(proj, side, wuk, wuv, l, nseq=nb_l, t_q=t_l, row_blk0=rows_c // t_l,
                      cache=(cache_ckv, ckr_pad), prev=o_mla)
        o_gla, st = _gla(proj, wgate, bgate, glg, l, nseq=nb_c, t=t_c, row_blk0=0, want_state=True)
        st_out.append(st)
        (o_gla,) = _gla(proj, wgate, bgate, glg, l, nseq=nb_l, t=t_l, row_blk0=rows_c // t_l,
                        state=state_gla, prev=o_gla)
        o_fft = _fft(proj, *dft_c, nseq=nb_c, t=t_c, row_blk0=0)
        o_fft = _fft(proj, *dft_l, nseq=nb_l, t=t_l, row_blk0=rows_c // t_l, prev=o_fft)

        x1, h2, aff_t = _mix(x, o_mla, o_gla, o_fft, proj, mods, n2, wm, wgo, wfo, wo, rwt, l,
                             nblk_ctx, blk_per_lat)

        aff_c = jnp.swapaxes(aff_t[:, :rows_c].reshape(N_EXPERTS, nb_c, t_c), 0, 1)
        aff_l = jnp.swapaxes(aff_t[:, rows_c:].reshape(N_EXPERTS, nb_l, t_l), 0, 1)
        slot_c = _route(aff_c.reshape(nb_c * N_EXPERTS, t_c), cap_c).reshape(nb_c, N_EXPERTS, t_c)
        slot_l = _route(aff_l.reshape(nb_l * N_EXPERTS, t_l), cap_l).reshape(nb_l, N_EXPERTS, t_l)

        xg, g = _gather(h2, slot_c, aff_c, None, None, nseq=nb_c, t=t_c, cap=cap_c, row_blk0=0,
                        slot_blk0=0, n_slots=n_slots)
        xg, g = _gather(h2, slot_l, aff_l, xg, g, nseq=nb_l, t=t_l, cap=cap_l, row_blk0=rows_c // t_l,
                        slot_blk0=nb_c * cap_c // cap_l, n_slots=n_slots)
        y = _experts(xg, g, exp_w1, exp_w3, exp_w2, l)
        x2 = _scatter(x1, y, jnp.swapaxes(slot_c, 1, 2), mods, None, l, nseq=nb_c, t=t_c, cap=cap_c,
                      row_blk0=0, slot_blk0=0, mod_row0=0, mod_step=0)
        x = _scatter(x1, y, jnp.swapaxes(slot_l, 1, 2), mods, x2, l, nseq=nb_l, t=t_l, cap=cap_l,
                     row_blk0=rows_c // t_l, slot_blk0=nb_c * cap_c // cap_l, mod_row0=1, mod_step=1)

    fg = final_norm_g.reshape(1, D)
    y_prompt = _final_norm(x, fg, row_blk0=0, nblk=nblk_ctx).reshape(nb_c, t_c, D)
    y_sample = _final_norm(x, fg, row_blk0=nblk_ctx, nblk=rows_l // ROW_TILE).reshape(nb_l, t_l, D)
    return (y_prompt, y_sample, jnp.stack(ckv_out, axis=1), jnp.stack(kr_out, axis=1),
            jnp.stack(st_out, axis=1))
```

```python
import functools
import math

import jax
import jax.numpy as jnp
import numpy as np
from jax import lax
from jax.experimental import pallas as pl
from jax.experimental.pallas import tpu as pltpu

F32 = jnp.float32
BF16 = jnp.bfloat16

D_MODEL = 1024
DEPTH = 4
GRID_W = 64
H_MLA = 8
D_NOPE = 64
D_ROPE = 32
D_V = 64
KV_RANK = 256
ROPE_BASE = 10000.0
MLA_SCALE = (D_NOPE + D_ROPE) ** -0.5
H_GLA = 4
GLA_DK = 128
GLA_DV = 128
GLA_GATE_RANK = 16
GLA_TAU = 16.0
GLA_CHUNK = 64
FFT_GROUPS = 4
FFT_GROUP_W = 128
FFT_WIDTH = FFT_GROUPS * FFT_GROUP_W
N_EXPERTS = 16
D_EXPERT = 2048
CAPACITY_FACTOR = 2
N_MOD = 6
EPS = 1e-6

LANE = 128
ROW_TILE = 512
HEAD_PAD = 128
KR_OFF = D_NOPE
N_CHUNK = 7
MOD_ROWS = 16
EXPERT_F_TILE = 512
EXPERT_ROW_TILE = 512
ROUTE_REFINE_STEPS = 16
Q_PRESCALE = MLA_SCALE * math.log2(math.e)
VMEM_LIMIT = 56 * 1024 * 1024


def _cparams(n_axes):
    return pltpu.CompilerParams(dimension_semantics=("arbitrary",) * n_axes, vmem_limit_bytes=VMEM_LIMIT)


def _dot(a, b):
    return jnp.dot(a, b, preferred_element_type=F32)


def _dot_nt(a, b):
    return lax.dot_general(a, b, (((1,), (1,)), ((), ())), preferred_element_type=F32)


def _dot_tn(a, b):
    return lax.dot_general(a, b, (((0,), (0,)), ((), ())), preferred_element_type=F32)


def _sigmoid(x):
    return 1.0 / (1.0 + jnp.exp(-x))


def _rms(x, g):
    return x * lax.rsqrt(jnp.mean(x * x, axis=-1, keepdims=True) + EPS) * g


def _mods_kernel(c_ref, w_ref, b_ref, o_ref):
    c = c_ref[...]
    s = (c * _sigmoid(c)).astype(BF16)
    o_ref[...] = _dot(s, w_ref[...].astype(BF16)) + b_ref[...]


def _mods(c_all, ada_w, ada_b):
    D = D_MODEL
    return pl.pallas_call(
        _mods_kernel,
        grid=(DEPTH, N_MOD),
        in_specs=[
            pl.BlockSpec((MOD_ROWS, D), lambda l, k: (0, 0)),
            pl.BlockSpec((None, D, D), lambda l, k: (l, 0, k)),
            pl.BlockSpec((None, None, 1, D), lambda l, k: (l, k, 0, 0)),
        ],
        out_specs=pl.BlockSpec((None, None, MOD_ROWS, D), lambda l, k: (l, k, 0, 0)),
        out_shape=jax.ShapeDtypeStruct((DEPTH, N_MOD, MOD_ROWS, D), F32),
        compiler_params=_cparams(2),
        name="adaln_mods",
    )(c_all, ada_w, ada_b.reshape(DEPTH, N_MOD, 1, D))


def _mod_spec(l, mod_row0, rows_per_mod):
    if rows_per_mod is None:
        return pl.BlockSpec((None, N_MOD, None, 1, D_MODEL), lambda i: (l, 0, mod_row0, 0, 0))
    return pl.BlockSpec((None, N_MOD, None, 1, D_MODEL), lambda i: (l, 0, mod_row0 + i // rows_per_mod, 0, 0))


def _inproj_kernel(*refs, has_rope):
    if has_rope:
        x_ref, mod_ref, n1_ref, kvg_ref, w_ref, rope_ref, proj_ref, ckv_ref, krp_ref = refs
        cos, sin_up, sin_dn = rope_ref[0], rope_ref[1], rope_ref[2]

        def rope(a):
            return a * cos + pltpu.roll(a, LANE - 8, 1) * sin_up + pltpu.roll(a, 8, 1) * sin_dn
    else:
        x_ref, mod_ref, n1_ref, kvg_ref, w_ref, proj_ref, ckv_ref, krp_ref = refs

        def rope(a):
            return a

    x = x_ref[...]
    h = _rms(x, n1_ref[...]) * (1.0 + mod_ref[1]) + mod_ref[0]
    hb = h.astype(BF16)

    acc = _dot(hb, w_ref[0])
    for hd in range(H_MLA):
        sl = slice(hd * HEAD_PAD, (hd + 1) * HEAD_PAD)
        proj_ref[0, :, sl] = (rope(acc[:, sl]) * Q_PRESCALE).astype(BF16)

    acc = _dot(hb, w_ref[1])
    ckv_ref[...] = _rms(acc[:, :KV_RANK], kvg_ref[...])
    krp_ref[...] = rope(acc[:, KV_RANK:KV_RANK + HEAD_PAD])
    proj_ref[1] = acc.astype(BF16)

    for c in (2, 3):
        proj_ref[c] = _dot(hb, w_ref[c]).astype(BF16)
    for c in (4, 5, 6):
        proj_ref[c] = _sigmoid(_dot(hb, w_ref[c])).astype(BF16)


def _inproj(x, mods, n1, kvg, w_in_r, l, *, mod_row0, rows_per_mod, rope_tab=None):
    R, D = x.shape
    in_specs = [
        pl.BlockSpec((ROW_TILE, D), lambda i: (i, 0)),
        _mod_spec(l, mod_row0, rows_per_mod),
        pl.BlockSpec((None, 1, D), lambda i: (l, 0, 0)),
        pl.BlockSpec((None, 1, KV_RANK), lambda i: (l, 0, 0)),
        pl.BlockSpec((None, N_CHUNK, D, D), lambda i: (l, 0, 0, 0), pipeline_mode=pl.Buffered(1)),
    ]
    args = [x, mods, n1, kvg, w_in_r]
    if rope_tab is not None:
        nper = rope_tab.shape[0]
        in_specs.append(pl.BlockSpec((None, 3, ROW_TILE, HEAD_PAD), lambda i: (i % nper, 0, 0, 0)))
        args.append(rope_tab)
    return pl.pallas_call(
        functools.partial(_inproj_kernel, has_rope=rope_tab is not None),
        grid=(R // ROW_TILE,),
        in_specs=in_specs,
        out_specs=[
            pl.BlockSpec((N_CHUNK, ROW_TILE, D), lambda i: (0, i, 0)),
            pl.BlockSpec((ROW_TILE, KV_RANK), lambda i: (i, 0)),
            pl.BlockSpec((ROW_TILE, HEAD_PAD), lambda i: (i, 0)),
        ],
        out_shape=[
            jax.ShapeDtypeStruct((N_CHUNK, R, D), BF16),
            jax.ShapeDtypeStruct((R, KV_RANK), F32),
            jax.ShapeDtypeStruct((R, HEAD_PAD), F32),
        ],
        compiler_params=_cparams(1),
        name="inproj",
    )(*args)


def _attn_kernel(*refs, t_q, has_cache, q_blk):
    if has_cache:
        q_ref, ckv_ref, krp_ref, cckv_ref, ckr_ref, wuk_ref, wuv_ref, o_ref = refs
        ckv = jnp.concatenate([cckv_ref[...], ckv_ref[...]], axis=0)
        krp = jnp.concatenate([ckr_ref[...], krp_ref[...]], axis=0)
    else:
        q_ref, ckv_ref, krp_ref, wuk_ref, wuv_ref, o_ref = refs
        ckv = ckv_ref[...]
        krp = krp_ref[...]
    ckvb = ckv.astype(BF16)
    k_all = _dot(ckvb, wuk_ref[...])
    v_all = _dot(ckvb, wuv_ref[...]).astype(BF16)
    for pair in range(H_MLA // 2):
        k_pair = [(k_all[:, hd * HEAD_PAD:(hd + 1) * HEAD_PAD] + krp).astype(BF16)
                  for hd in (2 * pair, 2 * pair + 1)]
        for qb in range(t_q // q_blk):
            rows = slice(qb * q_blk, (qb + 1) * q_blk)
            o_pair = None
            for j, hd in enumerate((2 * pair, 2 * pair + 1)):
                sl = slice(hd * HEAD_PAD, (hd + 1) * HEAD_PAD)
                s = _dot_nt(q_ref[rows, sl], k_pair[j])
                p = jnp.exp2(s - jnp.max(s, axis=-1, keepdims=True))
                den = jnp.sum(p, axis=-1, keepdims=True)
                o_h = _dot(p.astype(BF16), v_all[:, sl]) / den
                o_pair = o_h if o_pair is None else o_pair + o_h
            o_ref[rows, pair * LANE:(pair + 1) * LANE] = o_pair.astype(BF16)


def _attn(proj, ckv, krp, wuk, wuv, l, *, t_q, cache=None):
    R = ckv.shape[0]
    in_specs = [
        pl.BlockSpec((None, t_q, D_MODEL), lambda b: (0, b, 0)),
        pl.BlockSpec((t_q, KV_RANK), lambda b: (b, 0)),
        pl.BlockSpec((t_q, HEAD_PAD), lambda b: (b, 0)),
    ]
    args = [proj, ckv, krp]
    if cache is not None:
        cckv, ckr = cache
        past = cckv.shape[2]
        in_specs += [
            pl.BlockSpec((None, None, past, KV_RANK), lambda b: (b, l, 0, 0)),
            pl.BlockSpec((None, None, past, HEAD_PAD), lambda b: (b, l, 0, 0)),
        ]
        args += [cckv, ckr]
    in_specs += [
        pl.BlockSpec((None, KV_RANK, H_MLA * HEAD_PAD), lambda b: (l, 0, 0)),
        pl.BlockSpec((None, KV_RANK, H_MLA * HEAD_PAD), lambda b: (l, 0, 0)),
    ]
    args += [wuk, wuv]
    return pl.pallas_call(
        functools.partial(_attn_kernel, t_q=t_q, has_cache=cache is not None, q_blk=min(t_q, 256)),
        grid=(R // t_q,),
        in_specs=in_specs,
        out_specs=pl.BlockSpec((t_q, H_MLA * D_V), lambda b: (b, 0)),
        out_shape=jax.ShapeDtypeStruct((R, H_MLA * D_V), BF16),
        compiler_params=_cparams(1),
        name="mla_attn",
    )(*args)


def _gla_kernel(*refs, t, has_state, want_state):
    refs = list(refs)
    q_ref, k_ref, v_ref, r_ref, glr_ref, wg_ref, bg_ref, g_ref = refs[:8]
    pos = 8
    s0_ref = None
    if has_state:
        s0_ref = refs[pos]
        pos += 1
    o_ref = refs[pos]
    pos += 1
    so_ref = None
    if want_state:
        so_ref = refs[pos]
        pos += 1
    b_ref, of_ref, ob_ref, st_ref = refs[pos:pos + 4]

    CH = GLA_CHUNK
    BLK = 256
    HW = H_GLA * GLA_DK
    n_chunks = t // CH

    pre = _dot(glr_ref[...], wg_ref[...]) + bg_ref[...]
    la = (jnp.minimum(pre, 0.0) - jnp.log(1.0 + jnp.exp(-jnp.abs(pre)))) * (1.0 / GLA_TAU)
    la_hi = la.astype(BF16)
    la_lo = (la - la_hi.astype(F32)).astype(BF16)
    ri = lax.broadcasted_iota(jnp.int32, (BLK, BLK), 0)
    ci = lax.broadcasted_iota(jnp.int32, (BLK, BLK), 1)
    same = (ri // CH) == (ci // CH)
    tri_f = jnp.where(same & (ci <= ri), 1.0, 0.0).astype(BF16)
    tri_b = jnp.where(same & (ci >= ri), 1.0, 0.0).astype(BF16)
    for blk in range(t // BLK):
        rows = slice(blk * BLK, (blk + 1) * BLK)
        hi, lo = la_hi[rows], la_lo[rows]
        b_ref[rows, :HW] = _dot(tri_f, hi[:, :HW]) + _dot(tri_f, lo[:, :HW])
        b_ref[rows, HW:] = _dot(tri_b, hi[:, HW:]) + _dot(tri_b, lo[:, HW:])

    for h in range(H_GLA):
        for d in range(2):
            if has_state:
                st_ref[2 * h + d] = s0_ref[d, h].T
            else:
                st_ref[2 * h + d] = jnp.zeros((GLA_DV, GLA_DK), F32)

    rc = lax.broadcasted_iota(jnp.int32, (CH, CH), 0)
    cc = lax.broadcasted_iota(jnp.int32, (CH, CH), 1)
    masks = (cc <= rc, cc >= rc)
    scale = GLA_DK ** -0.5

    def chain(h, d, c):
        if isinstance(c, int):
            rows = slice(c * CH, (c + 1) * CH)
        else:
            rows = pl.ds(pl.multiple_of(c * CH, CH), CH)
        cols = slice(h * GLA_DK, (h + 1) * GLA_DK)
        b = b_ref[rows, d * HW + h * GLA_DK:d * HW + (h + 1) * GLA_DK]
        b_last = b[CH - 1:CH] if d == 0 else b[0:1]
        q = q_ref[rows, cols].astype(F32) * scale
        k = k_ref[rows, cols].astype(F32)
        v = v_ref[rows, cols]
        q_t = (q * jnp.exp(b)).astype(BF16)
        k_t = (k * jnp.exp(-b)).astype(BF16)
        k_end = (k * jnp.exp(b_last - b)).astype(BF16)
        st = st_ref[2 * h + d]
        att = jnp.where(masks[d], _dot_nt(q_t, k_t), 0.0).astype(BF16)
        o = _dot(att, v) + _dot_nt(q_t, st.astype(BF16))
        if d == 0:
            of_ref[rows, cols] = o
        else:
            ob_ref[rows, cols] = o
        st_ref[2 * h + d] = st * jnp.exp(b_last) + _dot_tn(v, k_end)

    def step(n):
        for h in range(H_GLA):
            chain(h, 0, n)
            chain(h, 1, n_chunks - 1 - n)

    if n_chunks <= 4:
        for n in range(n_chunks):
            step(n)
    else:
        def body(n, carry):
            step(n)
            return carry
        lax.fori_loop(0, n_chunks, body, 0, unroll=2)

    if want_state:
        for h in range(H_GLA):
            for d in range(2):
                so_ref[d, h] = st_ref[2 * h + d].T

    g = g_ref[...]
    for h in range(H_GLA):
        cols = slice(h * GLA_DV, (h + 1) * GLA_DV)
        o = _rms(of_ref[:, cols] + ob_ref[:, cols], g)
        r = r_ref[:, cols].astype(F32)
        o_ref[:, cols] = (o * (r * _sigmoid(r))).astype(BF16)


def _gla(proj, glr_w, glr_b, gla_g, l, *, t, state=None, want_state=False):
    R = proj.shape[1]
    HW = H_GLA * GLA_DK
    nseq = R // t

    def spec(chunk, half):
        return pl.BlockSpec((None, t, HW), lambda b: (chunk, b, half))

    in_specs = [
        spec(2, 0), spec(2, 1), spec(3, 0), spec(3, 1),
        pl.BlockSpec((None, t, LANE), lambda b: (1, b, 3)),
        pl.BlockSpec((None, LANE, 2 * HW), lambda b: (l, 0, 0)),
        pl.BlockSpec((None, 1, 2 * HW), lambda b: (l, 0, 0)),
        pl.BlockSpec((None, 1, GLA_DV), lambda b: (l, 0, 0)),
    ]
    args = [proj, proj, proj, proj, proj, glr_w, glr_b, gla_g]
    if state is not None:
        in_specs.append(pl.BlockSpec((None, None, 2, H_GLA, GLA_DK, GLA_DV), lambda b: (b, l, 0, 0, 0, 0)))
        args.append(state)
    out_specs = [pl.BlockSpec((t, HW), lambda b: (b, 0))]
    out_shape = [jax.ShapeDtypeStruct((R, HW), BF16)]
    if want_state:
        out_specs.append(pl.BlockSpec((None, 2, H_GLA, GLA_DK, GLA_DV), lambda b: (b, 0, 0, 0, 0)))
        out_shape.append(jax.ShapeDtypeStruct((nseq, 2, H_GLA, GLA_DK, GLA_DV), F32))
    return pl.pallas_call(
        functools.partial(_gla_kernel, t=t, has_state=state is not None, want_state=want_state),
        grid=(nseq,),
        in_specs=in_specs,
        out_specs=out_specs,
        out_shape=out_shape,
        scratch_shapes=[pltpu.VMEM((t, 2 * HW), F32), pltpu.VMEM((t, HW), F32), pltpu.VMEM((t, HW), F32),
                        pltpu.VMEM((2 * H_GLA, GLA_DV, GLA_DK), F32)],
        compiler_params=_cparams(1),
        name="gla",
    )(*args)


def _fft_kernel(u_ref, cw_ref, sw_ref, ct_ref, o_ref):
    u = u_ref[...]
    a = _dot(u, cw_ref[...]).astype(BF16)
    b = _dot(u, sw_ref[...]).astype(BF16)
    ab = jnp.concatenate([a, b], axis=0)
    o_ref[...] = _dot(ct_ref[...], ab).astype(BF16)


def _fft(proj, cw, sw, ct, *, t):
    R = proj.shape[1]
    W = FFT_WIDTH
    return pl.pallas_call(
        _fft_kernel,
        grid=(R // t,),
        in_specs=[
            pl.BlockSpec((None, t, W), lambda b: (1, b, 1)),
            pl.BlockSpec((W, W), lambda b: (0, 0)),
            pl.BlockSpec((W, W), lambda b: (0, 0)),
            pl.BlockSpec((t, 2 * t), lambda b: (0, 0)),
        ],
        out_specs=pl.BlockSpec((t, W), lambda b: (b, 0)),
        out_shape=jax.ShapeDtypeStruct((R, W), BF16),
        compiler_params=_cparams(1),
        name="fourier_mix",
    )(proj, cw, sw, ct)


def _mix_kernel(x_ref, om_ref, og_ref, of_ref, g0_ref, g1_ref, g2_ref, mod_ref, n2_ref,
                wm_ref, wg_ref, wf_ref, wo_ref, rw_ref, x1_ref, h2_ref, aff_ref, *, seqs_per_tile):
    ym = _dot(om_ref[...], wm_ref[...])
    yg = _dot(og_ref[...], wg_ref[...])
    yf = _dot(of_ref[...], wf_ref[...])
    merged = g0_ref[...].astype(F32) * ym + g1_ref[...].astype(F32) * yg + g2_ref[...].astype(F32) * yf
    z = _dot(merged.astype(BF16), wo_ref[...])
    x1 = x_ref[...] + mod_ref[2] * z
    x1_ref[...] = x1
    h2 = _rms(x1, n2_ref[...]) * (1.0 + mod_ref[4]) + mod_ref[3]
    h2_ref[...] = h2.astype(BF16)
    h_hi = h2.astype(BF16)
    h_lo = (h2 - h_hi.astype(F32)).astype(BF16)
    lg = _dot(h_hi, rw_ref[...])
    lg = lg[:, :LANE] + lg[:, LANE:] + _dot(h_lo, rw_ref[:, :LANE])
    logits = lg.T[:N_EXPERTS]
    m = jnp.max(logits, axis=0, keepdims=True)
    p = jnp.exp(logits - m)
    aff = p / jnp.sum(p, axis=0, keepdims=True)
    if seqs_per_tile == 0:
        aff_ref[...] = aff
    else:
        t = ROW_TILE // seqs_per_tile
        for s in range(seqs_per_tile):
            aff_ref[s] = aff[:, s * t:(s + 1) * t]


def _mix(x, o_mla, o_gla, o_fft, proj, mods, n2, wm, wg, wf, wo, rwt, l, *, t, mod_row0, rows_per_mod):
    R, D = x.shape
    HW = H_MLA * D_V
    nseq = R // t

    def wspec(k):
        return pl.BlockSpec((None, k, D), lambda i: (l, 0, 0))

    if t <= ROW_TILE:
        seqs_per_tile = ROW_TILE // t
        aff_spec = pl.BlockSpec((seqs_per_tile, N_EXPERTS, t), lambda i: (i, 0, 0))
    else:
        seqs_per_tile = 0
        tiles = t // ROW_TILE
        aff_spec = pl.BlockSpec((None, N_EXPERTS, ROW_TILE), lambda i: (i // tiles, 0, i % tiles))
    return pl.pallas_call(
        functools.partial(_mix_kernel, seqs_per_tile=seqs_per_tile),
        grid=(R // ROW_TILE,),
        in_specs=[
            pl.BlockSpec((ROW_TILE, D), lambda i: (i, 0)),
            pl.BlockSpec((ROW_TILE, HW), lambda i: (i, 0)),
            pl.BlockSpec((ROW_TILE, HW), lambda i: (i, 0)),
            pl.BlockSpec((ROW_TILE, HW), lambda i: (i, 0)),
            pl.BlockSpec((None, ROW_TILE, D), lambda i: (4, i, 0)),
            pl.BlockSpec((None, ROW_TILE, D), lambda i: (5, i, 0)),
            pl.BlockSpec((None, ROW_TILE, D), lambda i: (6, i, 0)),
            _mod_spec(l, mod_row0, rows_per_mod),
            pl.BlockSpec((None, 1, D), lambda i: (l, 0, 0)),
            wspec(HW), wspec(HW), wspec(HW), wspec(D),
            pl.BlockSpec((None, D, 2 * LANE), lambda i: (l, 0, 0)),
        ],
        out_specs=[
            pl.BlockSpec((ROW_TILE, D), lambda i: (i, 0)),
            pl.BlockSpec((ROW_TILE, D), lambda i: (i, 0)),
            aff_spec,
        ],
        out_shape=[
            jax.ShapeDtypeStruct((R, D), F32),
            jax.ShapeDtypeStruct((R, D), BF16),
            jax.ShapeDtypeStruct((nseq, N_EXPERTS, t), F32),
        ],
        compiler_params=_cparams(1),
        name="mix_out",
    )(x, o_mla, o_gla, o_fft, proj, proj, proj, mods, n2, wm, wg, wf, wo, rwt)


def _route_kernel(a_ref, slot_ref, *, cap):
    a = a_ref[...]
    rows, t = a.shape

    def count_ge(x):
        return jnp.sum(jnp.where(a >= x, 1.0, 0.0), axis=1, keepdims=True)

    thr = jnp.zeros((rows, 1), jnp.int32)
    for bit in range(30, -1, -1):
        cand = thr | (1 << bit)
        thr = jnp.where(count_ge(pltpu.bitcast(cand, F32)) >= cap, cand, thr)
    lo = pltpu.bitcast(thr, F32)
    hi = pltpu.bitcast(thr + 1, F32)
    for _ in range(ROUTE_REFINE_STEPS):
        mid = lo + (hi - lo) * 0.5
        keep = count_ge(mid) >= cap
        lo = jnp.where(keep, mid, lo)
        hi = jnp.where(keep, hi, mid)
    gt = a > lo
    eq = a == lo
    need = cap - jnp.sum(jnp.where(gt, 1.0, 0.0), axis=1, keepdims=True)
    ri = lax.broadcasted_iota(jnp.int32, (t, t), 0)
    ci = lax.broadcasted_iota(jnp.int32, (t, t), 1)
    before = jnp.where(ri < ci, 1.0, 0.0).astype(BF16)
    eq_rank = _dot(jnp.where(eq, 1.0, 0.0).astype(BF16), before)
    sel = gt | (eq & (eq_rank < need))
    slot = _dot(jnp.where(sel, 1.0, 0.0).astype(BF16), before)
    slot_ref[...] = jnp.where(sel, slot, -1.0)


def _route(a, cap):
    return pl.pallas_call(
        functools.partial(_route_kernel, cap=cap),
        out_shape=jax.ShapeDtypeStruct(a.shape, F32),
        compiler_params=pltpu.CompilerParams(vmem_limit_bytes=VMEM_LIMIT),
        name="route",
    )(a)


def _gather_kernel(h_ref, slot_ref, aff_ref, xg_ref, g_ref, p_ref, *, cap):
    t = h_ref.shape[0]
    ci = lax.broadcasted_iota(jnp.int32, (cap, t), 0).astype(F32)
    for e in range(N_EXPERTS):
        hit = ci == slot_ref[e:e + 1, :]
        p_ref[e * cap:(e + 1) * cap, :] = jnp.where(hit, 1.0, 0.0).astype(BF16)
        g_ref[e] = jnp.sum(jnp.where(hit, aff_ref[e:e + 1, :], 0.0), axis=1, keepdims=True)
    xg = _dot(p_ref[...], h_ref[...])
    for e in range(N_EXPERTS):
        xg_ref[e] = xg[e * cap:(e + 1) * cap].astype(BF16)


def _gather(h2, slot, aff, *, t, cap):
    R, D = h2.shape
    nseq = R // t
    return pl.pallas_call(
        functools.partial(_gather_kernel, cap=cap),
        grid=(nseq,),
        in_specs=[
            pl.BlockSpec((t, D), lambda b: (b, 0)),
            pl.BlockSpec((None, N_EXPERTS, t), lambda b: (b, 0, 0)),
            pl.BlockSpec((None, N_EXPERTS, t), lambda b: (b, 0, 0)),
        ],
        out_specs=[
            pl.BlockSpec((N_EXPERTS, cap, D), lambda b: (0, b, 0)),
            pl.BlockSpec((N_EXPERTS, cap, 1), lambda b: (0, b, 0)),
        ],
        out_shape=[
            jax.ShapeDtypeStruct((N_EXPERTS, nseq * cap, D), BF16),
            jax.ShapeDtypeStruct((N_EXPERTS, nseq * cap, 1), F32),
        ],
        scratch_shapes=[pltpu.VMEM((N_EXPERTS * cap, t), BF16)],
        compiler_params=_cparams(1),
        name="moe_gather",
    )(h2, slot, aff)


def _expert_kernel(xa_ref, xb_ref, ga_ref, gb_ref, w1_ref, w3_ref, w2_ref, ya_ref, yb_ref, acc_ref):
    f = pl.program_id(1)
    nf = pl.num_programs(1)
    na = xa_ref.shape[0]
    groups = ((xa_ref, ga_ref, ya_ref, 0), (xb_ref, gb_ref, yb_ref, na))

    def body(first, last):
        w1 = w1_ref[...].astype(BF16)
        w3 = w3_ref[...].astype(BF16)
        w2 = w2_ref[...].astype(BF16)
        for x_ref, g_ref, y_ref, base in groups:
            n = x_ref.shape[0]
            sub = min(n, EXPERT_ROW_TILE)
            for r in range(n // sub):
                rows = slice(r * sub, (r + 1) * sub)
                arows = slice(base + r * sub, base + (r + 1) * sub)
                x = x_ref[rows, :]
                a = _dot(x, w1)
                gt = _dot(x, w3)
                hm = (a * _sigmoid(a) * gt).astype(BF16)
                contrib = _dot(hm, w2)
                if not first:
                    contrib = acc_ref[arows, :] + contrib
                if last:
                    y_ref[rows, :] = (contrib * g_ref[rows, :]).astype(BF16)
                else:
                    acc_ref[arows, :] = contrib

    @pl.when(f == 0)
    def _():
        body(True, False)

    @pl.when((f > 0) & (f < nf - 1))
    def _():
        body(False, False)

    @pl.when(f == nf - 1)
    def _():
        body(False, True)


def _experts(xa, xb, ga, gb, w1, w3, w2, l):
    E, na, D = xa.shape
    nb = xb.shape[1]
    FT = EXPERT_F_TILE
    assert D_EXPERT // FT >= 2

    def rows_spec(n, w):
        return pl.BlockSpec((None, n, w), lambda e, f: (e, 0, 0))

    return pl.pallas_call(
        _expert_kernel,
        grid=(E, D_EXPERT // FT),
        in_specs=[
            rows_spec(na, D), rows_spec(nb, D), rows_spec(na, 1), rows_spec(nb, 1),
            pl.BlockSpec((None, None, D, FT), lambda e, f: (l, e, 0, f)),
            pl.BlockSpec((None, None, D, FT), lambda e, f: (l, e, 0, f)),
            pl.BlockSpec((None, None, FT, D), lambda e, f: (l, e, f, 0)),
        ],
        out_specs=[rows_spec(na, D), rows_spec(nb, D)],
        out_shape=[jax.ShapeDtypeStruct((E, na, D), BF16), jax.ShapeDtypeStruct((E, nb, D), BF16)],
        scratch_shapes=[pltpu.VMEM((na + nb, D), F32)],
        compiler_params=_cparams(2),
        name="moe_experts",
    )(xa, xb, ga, gb, w1, w3, w2)


def _scatter_kernel(x_ref, y_ref, slot_ref, mod_ref, o_ref, *, cap):
    t = x_ref.shape[0]
    n = N_EXPERTS * cap
    er = lax.broadcasted_iota(jnp.int32, (N_EXPERTS, n), 0)
    ec = lax.broadcasted_iota(jnp.int32, (N_EXPERTS, n), 1)
    expand = jnp.where(ec // cap == er, 1.0, 0.0).astype(BF16)
    slot_wide = _dot_tn(slot_ref[...].astype(BF16), expand)
    want = (lax.broadcasted_iota(jnp.int32, (t, n), 1) % cap).astype(F32)
    pt = jnp.where(slot_wide == want, 1.0, 0.0).astype(BF16)
    y = jnp.concatenate([y_ref[e] for e in range(N_EXPERTS)], axis=0)
    o_ref[...] = x_ref[...] + mod_ref[5] * _dot(pt, y)


def _scatter(x1, y, slot, mods, l, *, t, cap, mod_row0, rows_per_mod):
    R, D = x1.shape
    return pl.pallas_call(
        functools.partial(_scatter_kernel, cap=cap),
        grid=(R // t,),
        in_specs=[
            pl.BlockSpec((t, D), lambda b: (b, 0)),
            pl.BlockSpec((N_EXPERTS, cap, D), lambda b: (0, b, 0)),
            pl.BlockSpec((None, N_EXPERTS, t), lambda b: (b, 0, 0)),
            _mod_spec(l, mod_row0, rows_per_mod),
        ],
        out_specs=pl.BlockSpec((t, D), lambda b: (b, 0)),
        out_shape=jax.ShapeDtypeStruct((R, D), F32),
        compiler_params=_cparams(1),
        name="moe_scatter",
    )(x1, y, slot, mods)


def _final_kernel(x_ref, g_ref, o_ref):
    o_ref[...] = _rms(x_ref[...], g_ref[...])


def _final_norm(x, g):
    R, D = x.shape
    return pl.pallas_call(
        _final_kernel,
        grid=(R // ROW_TILE,),
        in_specs=[pl.BlockSpec((ROW_TILE, D), lambda i: (i, 0)), pl.BlockSpec((1, D), lambda i: (0, 0))],
        out_specs=pl.BlockSpec((ROW_TILE, D), lambda i: (i, 0)),
        out_shape=jax.ShapeDtypeStruct((R, D), F32),
        compiler_params=_cparams(1),
        name="final_norm",
    )(x, g)


def _rope_tables(t_lat):
    half = D_ROPE // 2
    nfreq = half // 2
    pos = np.arange(t_lat)
    inv = 1.0 / (ROPE_BASE ** (np.arange(0, half, 2, dtype=np.float64) / half))
    cos = np.ones((t_lat, HEAD_PAD))
    s_up = np.zeros((t_lat, HEAD_PAD))
    s_dn = np.zeros((t_lat, HEAD_PAD))
    for part, p in enumerate((pos // GRID_W, pos % GRID_W)):
        ang = p[:, None].astype(np.float64) * inv
        o = KR_OFF + part * half
        cos[:, o:o + nfreq] = np.cos(ang)
        cos[:, o + nfreq:o + half] = np.cos(ang)
        s_up[:, o:o + nfreq] = -np.sin(ang)
        s_dn[:, o + nfreq:o + half] = np.sin(ang)
    tab = np.stack([cos, s_up, s_dn], axis=0).reshape(3, t_lat // ROW_TILE, ROW_TILE, HEAD_PAD)
    return jnp.asarray(np.moveaxis(tab, 1, 0), dtype=F32)


def _dft_mats(n, scale):
    k = np.arange(n)
    ang = (2.0 * np.pi / n) * ((k[:, None] * k[None, :]) % n)
    return np.cos(ang) * scale, np.sin(ang) * scale


def _dft_tables(t):
    cw, sw = _dft_mats(FFT_GROUP_W, FFT_GROUP_W ** -0.5)
    eye = np.eye(FFT_GROUPS)
    ct, st = _dft_mats(t, t ** -0.5)

    def const(a):
        return jnp.asarray(a, dtype=F32).astype(BF16)

    return const(np.kron(eye, cw)), const(np.kron(eye, sw)), const(np.concatenate([ct, -st], axis=1))


def _layout_w_in(w_in):
    L, D, _ = w_in.shape
    sizes = (H_MLA * (D_NOPE + D_ROPE), KV_RANK, D_ROPE, H_GLA * GLA_DK, H_GLA * GLA_DK, H_GLA * GLA_DV,
             H_GLA * GLA_DV, 2 * GLA_GATE_RANK, FFT_WIDTH, 3 * D_MODEL)
    offs = np.concatenate([[0], np.cumsum(sizes)])
    seg = [w_in[:, :, offs[i]:offs[i + 1]] for i in range(len(sizes))]
    q, ckv, kr, gq, gk, gv, gr, glr, uf, mg = seg
    q = q.reshape(L, D, H_MLA, D_NOPE + D_ROPE)
    q = jnp.pad(q, ((0, 0), (0, 0), (0, 0), (0, HEAD_PAD - D_NOPE - D_ROPE))).reshape(L, D, H_MLA * HEAD_PAD)
    kr = jnp.pad(kr, ((0, 0), (0, 0), (KR_OFF, HEAD_PAD - KR_OFF - D_ROPE)))
    glr = jnp.pad(glr, ((0, 0), (0, 0), (0, LANE - 2 * GLA_GATE_RANK)))
    cols = jnp.concatenate([q, ckv, kr, glr, uf, gq, gk, gv, gr, mg], axis=-1).astype(BF16)
    return jnp.moveaxis(cols.reshape(L, D, N_CHUNK, D), 2, 1)


def _layout_w_ukv(w_ukv):
    L = w_ukv.shape[0]
    w = w_ukv.reshape(L, KV_RANK, H_MLA, D_NOPE + D_V)
    k = jnp.pad(w[..., :D_NOPE], ((0, 0), (0, 0), (0, 0), (0, HEAD_PAD - D_NOPE)))
    v = w[..., D_NOPE:].reshape(L, KV_RANK, H_MLA // 2, 2, D_V)
    z = jnp.zeros_like(v[:, :, :, 0])
    v_even = jnp.concatenate([v[:, :, :, 0], z], axis=-1)
    v_odd = jnp.concatenate([z, v[:, :, :, 1]], axis=-1)
    v2 = jnp.stack([v_even, v_odd], axis=3)
    return (k.reshape(L, KV_RANK, H_MLA * HEAD_PAD).astype(BF16),
            v2.reshape(L, KV_RANK, H_MLA * HEAD_PAD).astype(BF16))


def _layout_gate(gate_w, gate_b):
    L = gate_w.shape[0]
    HW = H_GLA * GLA_DK
    wg = jnp.zeros((L, LANE, 2 * HW), F32)
    for d in range(2):
        wg = wg.at[:, d * GLA_GATE_RANK:(d + 1) * GLA_GATE_RANK, d * HW:(d + 1) * HW].set(gate_w[:, d])
    return wg.astype(BF16), gate_b.reshape(L, 1, 2 * HW)


def kernel(x_prompt, x_sample, cache_ckv, cache_krope, state_gla, c, c_ctx, ada_w, ada_b, norm1_g, norm2_g, w_in, mla_kv_norm_g, mla_w_ukv, gla_gate_w, gla_gate_b, gla_norm_g, w_mla_out, w_gla_out, w_fft_out, w_o, router_w, exp_w1, exp_w3, exp_w2, final_norm_g):
    nb_c, t_c, D = x_prompt.shape
    nb_l, t_l, _ = x_sample.shape
    assert (nb_c * t_c) % ROW_TILE == 0 and ROW_TILE % t_c == 0 and t_l % ROW_TILE == 0 and 1 + nb_l <= MOD_ROWS
    cap_c = CAPACITY_FACTOR * t_c // N_EXPERTS
    cap_l = CAPACITY_FACTOR * t_l // N_EXPERTS

    c_all = jnp.zeros((MOD_ROWS, D), F32).at[0].set(c_ctx).at[1:1 + nb_l].set(c)
    mods = _mods(c_all, ada_w, ada_b).reshape(DEPTH, N_MOD, MOD_ROWS, 1, D)
    rope_tab = _rope_tables(t_l)
    dft_c = _dft_tables(t_c)
    dft_l = _dft_tables(t_l)
    w_in_r = _layout_w_in(w_in)
    wuk, wuv = _layout_w_ukv(mla_w_ukv)
    wgate, bgate = _layout_gate(gla_gate_w, gla_gate_b)
    n1 = norm1_g.reshape(DEPTH, 1, D)
    n2 = norm2_g.reshape(DEPTH, 1, D)
    kvg = mla_kv_norm_g.reshape(DEPTH, 1, KV_RANK)
    glg = gla_norm_g.reshape(DEPTH, 1, GLA_DV)
    wm = w_mla_out.astype(BF16)
    wgo = w_gla_out.astype(BF16)
    wfo = w_fft_out.astype(BF16)
    wo = w_o.astype(BF16)
    rw_hi = router_w.astype(BF16)
    rw_lo = (router_w - rw_hi.astype(F32)).astype(BF16)
    rwt = jnp.zeros((DEPTH, D, 2 * LANE), BF16).at[:, :, :N_EXPERTS].set(rw_hi)
    rwt = rwt.at[:, :, LANE:LANE + N_EXPERTS].set(rw_lo)
    ckr_pad = jnp.pad(cache_krope, ((0, 0), (0, 0), (0, 0), (KR_OFF, HEAD_PAD - KR_OFF - D_ROPE)))

    ctx = dict(t=t_c, cap=cap_c, mod_row0=0, rows_per_mod=None)
    lat = dict(t=t_l, cap=cap_l, mod_row0=1, rows_per_mod=t_l // ROW_TILE)
    xs = [x_prompt.reshape(nb_c * t_c, D), x_sample.reshape(nb_l * t_l, D)]
    ckv_out, kr_out, st_out = [], [], []

    def pre_experts(x, grp, l, is_ctx):
        t, cap = grp["t"], grp["cap"]
        mod_kw = dict(mod_row0=grp["mod_row0"], rows_per_mod=grp["rows_per_mod"])
        proj, ckv, krp = _inproj(x, mods, n1, kvg, w_in_r, l, rope_tab=None if is_ctx else rope_tab, **mod_kw)
        if is_ctx:
            ckv_out.append(ckv.reshape(nb_c, t_c, KV_RANK))
            kr_out.append(krp[:, KR_OFF:KR_OFF + D_ROPE].reshape(nb_c, t_c, D_ROPE))
            o_mla = _attn(proj, ckv, krp, wuk, wuv, l, t_q=t)
            o_gla, st = _gla(proj, wgate, bgate, glg, l, t=t, want_state=True)
            st_out.append(st)
            o_fft = _fft(proj, *dft_c, t=t)
        else:
            o_mla = _attn(proj, ckv, krp, wuk, wuv, l, t_q=t, cache=(cache_ckv, ckr_pad))
            (o_gla,) = _gla(proj, wgate, bgate, glg, l, t=t, state=state_gla)
            o_fft = _fft(proj, *dft_l, t=t)
        rows_per_mod_mix = grp["rows_per_mod"]
        x1, h2, aff = _mix(x, o_mla, o_gla, o_fft, proj, mods, n2, wm, wgo, wfo, wo, rwt, l, t=t,
                           mod_row0=grp["mod_row0"], rows_per_mod=rows_per_mod_mix)
        nseq = aff.shape[0]
        slot = _route(aff.reshape(nseq * N_EXPERTS, t), cap).reshape(nseq, N_EXPERTS, t)
        xg, g = _gather(h2, slot, aff, t=t, cap=cap)
        return x1, slot, xg, g

    for l in range(DEPTH):
        x1_c, slot_c, xg_c, g_c = pre_experts(xs[0], ctx, l, True)
        x1_l, slot_l, xg_l, g_l = pre_experts(xs[1], lat, l, False)
        y_c, y_l = _experts(xg_c, xg_l, g_c, g_l, exp_w1, exp_w3, exp_w2, l)
        xs = [
            _scatter(x1_c, y_c, slot_c, mods, l, t=t_c, cap=cap_c, mod_row0=0, rows_per_mod=None),
            _scatter(x1_l, y_l, slot_l, mods, l, t=t_l, cap=cap_l, mod_row0=1, rows_per_mod=1),
        ]

    fg = final_norm_g.reshape(1, D)
    y_prompt = _final_norm(xs[0], fg).reshape(nb_c, t_c, D)
    y_sample = _final_norm(xs[1], fg).reshape(nb_l, t_l, D)
    return (y_prompt, y_sample, jnp.stack(ckv_out, axis=1), jnp.stack(kr_out, axis=1),
            jnp.stack(st_out, axis=1))
```

```python
import functools
import math

import jax
import jax.numpy as jnp
import numpy as np
from jax import lax
from jax.experimental import pallas as pl
from jax.experimental.pallas import tpu as pltpu

F32 = jnp.float32
BF16 = jnp.bfloat16

D_MODEL = 1024
DEPTH = 4
GRID_W = 64
H_MLA = 8
D_NOPE = 64
D_ROPE = 32
D_V = 64
KV_RANK = 256
ROPE_BASE = 10000.0
MLA_SCALE = (D_NOPE + D_ROPE) ** -0.5
H_GLA = 4
GLA_DK = 128
GLA_DV = 128
GLA_GATE_RANK = 16
GLA_TAU = 16.0
GLA_CHUNK = 64
FFT_GROUPS = 4
FFT_GROUP_W = 128
FFT_WIDTH = FFT_GROUPS * FFT_GROUP_W
N_EXPERTS = 16
D_EXPERT = 2048
CAPACITY_FACTOR = 2
N_MOD = 6
EPS = 1e-6

LANE = 128
ROW_TILE = 512
HEAD_PAD = 128
KR_OFF = D_NOPE
N_CHUNK = 7
MOD_ROWS = 16
EXPERT_F_TILE = 512
EXPERT_ROW_TILE = 512
ATTN_Q_TILE = 512
SHORT_SEQ_ROWS = 1024
ROUTE_REFINE_STEPS = 16
Q_PRESCALE = MLA_SCALE * math.log2(math.e)
VMEM_LIMIT = 56 * 1024 * 1024


def _cparams(n_axes):
    return pltpu.CompilerParams(dimension_semantics=("arbitrary",) * n_axes, vmem_limit_bytes=VMEM_LIMIT)


def _dot(a, b):
    return jnp.dot(a, b, preferred_element_type=F32)


def _dot_nt(a, b):
    return lax.dot_general(a, b, (((1,), (1,)), ((), ())), preferred_element_type=F32)


def _dot_tn(a, b):
    return lax.dot_general(a, b, (((0,), (0,)), ((), ())), preferred_element_type=F32)


def _sigmoid(x):
    return 1.0 / (1.0 + jnp.exp(-x))


def _rms(x, g):
    return x * lax.rsqrt(jnp.mean(x * x, axis=-1, keepdims=True) + EPS) * g


def _mods_kernel(c_ref, w_ref, b_ref, o_ref):
    c = c_ref[...]
    s = (c * _sigmoid(c)).astype(BF16)
    o_ref[...] = _dot(s, w_ref[...].astype(BF16)) + b_ref[...]


def _mods(c_all, ada_w, ada_b):
    D = D_MODEL
    return pl.pallas_call(
        _mods_kernel,
        grid=(DEPTH, N_MOD),
        in_specs=[
            pl.BlockSpec((MOD_ROWS, D), lambda l, k: (0, 0)),
            pl.BlockSpec((None, D, D), lambda l, k: (l, 0, k)),
            pl.BlockSpec((None, None, 1, D), lambda l, k: (l, k, 0, 0)),
        ],
        out_specs=pl.BlockSpec((None, None, MOD_ROWS, D), lambda l, k: (l, k, 0, 0)),
        out_shape=jax.ShapeDtypeStruct((DEPTH, N_MOD, MOD_ROWS, D), F32),
        compiler_params=_cparams(2),
        name="adaln_mods",
    )(c_all, ada_w, ada_b.reshape(DEPTH, N_MOD, 1, D))


def _mod_spec(l, mod_row0, rows_per_mod):
    if rows_per_mod is None:
        return pl.BlockSpec((None, N_MOD, None, 1, D_MODEL), lambda i: (l, 0, mod_row0, 0, 0))
    return pl.BlockSpec((None, N_MOD, None, 1, D_MODEL), lambda i: (l, 0, mod_row0 + i // rows_per_mod, 0, 0))


def _inproj_kernel(*refs, has_rope):
    if has_rope:
        x_ref, mod_ref, n1_ref, kvg_ref, w_ref, rope_ref, proj_ref, ckv_ref, krp_ref = refs
        cos, sin_up, sin_dn = rope_ref[0], rope_ref[1], rope_ref[2]

        def rope(a):
            return a * cos + pltpu.roll(a, LANE - 8, 1) * sin_up + pltpu.roll(a, 8, 1) * sin_dn
    else:
        x_ref, mod_ref, n1_ref, kvg_ref, w_ref, proj_ref, ckv_ref, krp_ref = refs

        def rope(a):
            return a

    x = x_ref[...]
    h = _rms(x, n1_ref[...]) * (1.0 + mod_ref[1]) + mod_ref[0]
    hb = h.astype(BF16)

    acc = _dot(hb, w_ref[0])
    for hd in range(H_MLA):
        sl = slice(hd * HEAD_PAD, (hd + 1) * HEAD_PAD)
        proj_ref[0, :, sl] = (rope(acc[:, sl]) * Q_PRESCALE).astype(BF16)

    acc = _dot(hb, w_ref[1])
    ckv_ref[...] = _rms(acc[:, :KV_RANK], kvg_ref[...])
    krp_ref[...] = rope(acc[:, KV_RANK:KV_RANK + HEAD_PAD])
    proj_ref[1] = acc.astype(BF16)

    for c in (2, 3):
        proj_ref[c] = _dot(hb, w_ref[c]).astype(BF16)
    for c in (4, 5, 6):
        proj_ref[c] = _sigmoid(_dot(hb, w_ref[c])).astype(BF16)


def _inproj(x, mods, n1, kvg, w_in_r, l, *, mod_row0, rows_per_mod, rope_tab=None):
    R, D = x.shape
    in_specs = [
        pl.BlockSpec((ROW_TILE, D), lambda i: (i, 0)),
        _mod_spec(l, mod_row0, rows_per_mod),
        pl.BlockSpec((None, 1, D), lambda i: (l, 0, 0)),
        pl.BlockSpec((None, 1, KV_RANK), lambda i: (l, 0, 0)),
        pl.BlockSpec((None, N_CHUNK, D, D), lambda i: (l, 0, 0, 0), pipeline_mode=pl.Buffered(1)),
    ]
    args = [x, mods, n1, kvg, w_in_r]
    if rope_tab is not None:
        nper = rope_tab.shape[0]
        in_specs.append(pl.BlockSpec((None, 3, ROW_TILE, HEAD_PAD), lambda i: (i % nper, 0, 0, 0)))
        args.append(rope_tab)
    return pl.pallas_call(
        functools.partial(_inproj_kernel, has_rope=rope_tab is not None),
        grid=(R // ROW_TILE,),
        in_specs=in_specs,
        out_specs=[
            pl.BlockSpec((N_CHUNK, ROW_TILE, D), lambda i: (0, i, 0)),
            pl.BlockSpec((ROW_TILE, KV_RANK), lambda i: (i, 0)),
            pl.BlockSpec((ROW_TILE, HEAD_PAD), lambda i: (i, 0)),
        ],
        out_shape=[
            jax.ShapeDtypeStruct((N_CHUNK, R, D), BF16),
            jax.ShapeDtypeStruct((R, KV_RANK), F32),
            jax.ShapeDtypeStruct((R, HEAD_PAD), F32),
        ],
        compiler_params=_cparams(1),
        name="inproj",
    )(*args)


def _attn_kernel(*refs, t_q, seqs, has_cache, q_blk):
    if has_cache:
        q_ref, ckv_ref, krp_ref, cckv_ref, ckr_ref, wuk_ref, wuv_ref, o_ref = refs
        ckv = jnp.concatenate([cckv_ref[...], ckv_ref[...]], axis=0)
        krp = jnp.concatenate([ckr_ref[...], krp_ref[...]], axis=0)
    else:
        q_ref, ckv_ref, krp_ref, wuk_ref, wuv_ref, o_ref = refs
        ckv = ckv_ref[...]
        krp = krp_ref[...]
    t_k = ckv.shape[0] // seqs
    ckvb = ckv.astype(BF16)
    k_all = _dot(ckvb, wuk_ref[...])
    v_all = _dot(ckvb, wuv_ref[...]).astype(BF16)
    for sq in range(seqs):
        keys = slice(sq * t_k, (sq + 1) * t_k)
        for pair in range(H_MLA // 2):
            k_pair = [(k_all[keys, hd * HEAD_PAD:(hd + 1) * HEAD_PAD] + krp[keys]).astype(BF16)
                      for hd in (2 * pair, 2 * pair + 1)]
            for qb in range(t_q // q_blk):
                rows = slice(sq * t_q + qb * q_blk, sq * t_q + (qb + 1) * q_blk)
                o_pair = None
                for j, hd in enumerate((2 * pair, 2 * pair + 1)):
                    sl = slice(hd * HEAD_PAD, (hd + 1) * HEAD_PAD)
                    s = _dot_nt(q_ref[rows, sl], k_pair[j])
                    p = jnp.exp2(s - jnp.max(s, axis=-1, keepdims=True))
                    den = jnp.sum(p, axis=-1, keepdims=True)
                    o_h = _dot(p.astype(BF16), v_all[keys, sl]) / den
                    o_pair = o_h if o_pair is None else o_pair + o_h
                o_ref[rows, pair * LANE:(pair + 1) * LANE] = o_pair.astype(BF16)


def _seqs_per_step(t, has_cache=False):
    return 1 if has_cache else max(1, SHORT_SEQ_ROWS // t)


def _attn(proj, ckv, krp, wuk, wuv, l, *, t_q, cache=None):
    R = ckv.shape[0]
    seqs = _seqs_per_step(t_q, cache is not None)
    t_q_seq, t_q = t_q, t_q * seqs
    in_specs = [
        pl.BlockSpec((None, t_q, D_MODEL), lambda b: (0, b, 0)),
        pl.BlockSpec((t_q, KV_RANK), lambda b: (b, 0)),
        pl.BlockSpec((t_q, HEAD_PAD), lambda b: (b, 0)),
    ]
    args = [proj, ckv, krp]
    if cache is not None:
        cckv, ckr = cache
        past = cckv.shape[2]
        in_specs += [
            pl.BlockSpec((None, None, past, KV_RANK), lambda b: (b, l, 0, 0)),
            pl.BlockSpec((None, None, past, HEAD_PAD), lambda b: (b, l, 0, 0)),
        ]
        args += [cckv, ckr]
    in_specs += [
        pl.BlockSpec((None, KV_RANK, H_MLA * HEAD_PAD), lambda b: (l, 0, 0)),
        pl.BlockSpec((None, KV_RANK, H_MLA * HEAD_PAD), lambda b: (l, 0, 0)),
    ]
    args += [wuk, wuv]
    return pl.pallas_call(
        functools.partial(_attn_kernel, t_q=t_q_seq, seqs=seqs, has_cache=cache is not None,
                          q_blk=min(t_q_seq, ATTN_Q_TILE)),
        grid=(R // t_q,),
        in_specs=in_specs,
        out_specs=pl.BlockSpec((t_q, H_MLA * D_V), lambda b: (b, 0)),
        out_shape=jax.ShapeDtypeStruct((R, H_MLA * D_V), BF16),
        compiler_params=_cparams(1),
        name="mla_attn",
    )(*args)


def _gla_kernel(*refs, t, has_state, want_state):
    refs = list(refs)
    q_ref, k_ref, v_ref, r_ref, glr_ref, wg_ref, bg_ref, g_ref = refs[:8]
    pos = 8
    s0_ref = None
    if has_state:
        s0_ref = refs[pos]
        pos += 1
    o_ref = refs[pos]
    pos += 1
    so_ref = None
    if want_state:
        so_ref = refs[pos]
        pos += 1
    b_ref, of_ref, ob_ref, st_ref = refs[pos:pos + 4]

    CH = GLA_CHUNK
    BLK = 256
    HW = H_GLA * GLA_DK

    pre = _dot(glr_ref[...], wg_ref[...]) + bg_ref[...]
    la = (jnp.minimum(pre, 0.0) - jnp.log(1.0 + jnp.exp(-jnp.abs(pre)))) * (1.0 / GLA_TAU)
    la_hi = la.astype(BF16)
    la_lo = (la - la_hi.astype(F32)).astype(BF16)
    ri = lax.broadcasted_iota(jnp.int32, (BLK, BLK), 0)
    ci = lax.broadcasted_iota(jnp.int32, (BLK, BLK), 1)
    same = (ri // CH) == (ci // CH)
    tri_f = jnp.where(same & (ci <= ri), 1.0, 0.0).astype(BF16)
    tri_b = jnp.where(same & (ci >= ri), 1.0, 0.0).astype(BF16)
    for blk in range(t // BLK):
        rows = slice(blk * BLK, (blk + 1) * BLK)
        hi, lo = la_hi[rows], la_lo[rows]
        b_ref[rows, :HW] = _dot(tri_f, hi[:, :HW]) + _dot(tri_f, lo[:, :HW])
        b_ref[rows, HW:] = _dot(tri_b, hi[:, HW:]) + _dot(tri_b, lo[:, HW:])

    for h in range(H_GLA):
        for d in range(2):
            if has_state:
                st_ref[2 * h + d] = s0_ref[d, h].T
            else:
                st_ref[2 * h + d] = jnp.zeros((GLA_DV, GLA_DK), F32)

    masks = (same & (ci <= ri), same & (ci >= ri))
    scale = GLA_DK ** -0.5
    CPB = BLK // CH
    zero_chunk = jnp.zeros((CH, GLA_DK), BF16)

    def chunk_diag(x):
        return jnp.concatenate(
            [jnp.concatenate([x[c * CH:(c + 1) * CH] if j == c else zero_chunk for j in range(CPB)], axis=1)
             for c in range(CPB)], axis=0)

    def unit(h, d, blk, vt):
        rows = slice(blk * BLK, (blk + 1) * BLK)
        cols = slice(h * GLA_DK, (h + 1) * GLA_DK)
        b = b_ref[rows, d * HW + h * GLA_DK:d * HW + (h + 1) * GLA_DK]
        last = [b[c * CH + CH - 1:c * CH + CH] if d == 0 else b[c * CH:c * CH + 1] for c in range(CPB)]
        b_last = jnp.concatenate([jnp.broadcast_to(x, (CH, GLA_DK)) for x in last], axis=0)
        q = q_ref[rows, cols].astype(F32) * scale
        k = k_ref[rows, cols].astype(F32)
        q_t = (q * jnp.exp(b)).astype(BF16)
        k_t = (k * jnp.exp(-b)).astype(BF16)
        k_end = (k * jnp.exp(b_last - b)).astype(BF16)
        att = jnp.where(masks[d], _dot_nt(q_t, k_t), 0.0).astype(BF16)
        ds = _dot(vt, chunk_diag(k_end))
        st = st_ref[2 * h + d]
        prev = [None] * CPB
        for c in (range(CPB) if d == 0 else range(CPB - 1, -1, -1)):
            prev[c] = st.astype(BF16)
            st = st * jnp.exp(last[c]) + ds[:, c * GLA_DK:(c + 1) * GLA_DK]
        st_ref[2 * h + d] = st
        o = _dot_nt(jnp.concatenate([att, chunk_diag(q_t)], axis=1),
                    jnp.concatenate([vt] + prev, axis=1))
        if d == 0:
            of_ref[rows, cols] = o
        else:
            ob_ref[rows, cols] = o

    n_blk = t // BLK
    for n in range(n_blk):
        for h in range(H_GLA):
            cols = slice(h * GLA_DV, (h + 1) * GLA_DV)
            for d, blk in ((0, n), (1, n_blk - 1 - n)):
                vt = v_ref[blk * BLK:(blk + 1) * BLK, cols].astype(F32).T.astype(BF16)
                unit(h, d, blk, vt)

    if want_state:
        for h in range(H_GLA):
            for d in range(2):
                so_ref[d, h] = st_ref[2 * h + d].T

    g = g_ref[...]
    for h in range(H_GLA):
        cols = slice(h * GLA_DV, (h + 1) * GLA_DV)
        o = _rms(of_ref[:, cols] + ob_ref[:, cols], g)
        r = r_ref[:, cols].astype(F32)
        o_ref[:, cols] = (o * (r * _sigmoid(r))).astype(BF16)


def _gla(proj, glr_w, glr_b, gla_g, l, *, t, state=None, want_state=False):
    R = proj.shape[1]
    HW = H_GLA * GLA_DK
    nseq = R // t

    def spec(chunk, half):
        return pl.BlockSpec((None, t, HW), lambda b: (chunk, b, half))

    in_specs = [
        spec(2, 0), spec(2, 1), spec(3, 0), spec(3, 1),
        pl.BlockSpec((None, t, LANE), lambda b: (1, b, 3)),
        pl.BlockSpec((None, LANE, 2 * HW), lambda b: (l, 0, 0)),
        pl.BlockSpec((None, 1, 2 * HW), lambda b: (l, 0, 0)),
        pl.BlockSpec((None, 1, GLA_DV), lambda b: (l, 0, 0)),
    ]
    args = [proj, proj, proj, proj, proj, glr_w, glr_b, gla_g]
    if state is not None:
        in_specs.append(pl.BlockSpec((None, None, 2, H_GLA, GLA_DK, GLA_DV), lambda b: (b, l, 0, 0, 0, 0)))
        args.append(state)
    out_specs = [pl.BlockSpec((t, HW), lambda b: (b, 0))]
    out_shape = [jax.ShapeDtypeStruct((R, HW), BF16)]
    if want_state:
        out_specs.append(pl.BlockSpec((None, 2, H_GLA, GLA_DK, GLA_DV), lambda b: (b, 0, 0, 0, 0)))
        out_shape.append(jax.ShapeDtypeStruct((nseq, 2, H_GLA, GLA_DK, GLA_DV), F32))
    return pl.pallas_call(
        functools.partial(_gla_kernel, t=t, has_state=state is not None, want_state=want_state),
        grid=(nseq,),
        in_specs=in_specs,
        out_specs=out_specs,
        out_shape=out_shape,
        scratch_shapes=[pltpu.VMEM((t, 2 * HW), F32), pltpu.VMEM((t, HW), F32), pltpu.VMEM((t, HW), F32),
                        pltpu.VMEM((2 * H_GLA, GLA_DV, GLA_DK), F32)],
        compiler_params=_cparams(1),
        name="gla",
    )(*args)


def _fft_kernel(u_ref, cw_ref, sw_ref, ct_ref, o_ref, *, t):
    u = u_ref[...]
    a = _dot(u, cw_ref[...]).astype(BF16)
    b = _dot(u, sw_ref[...]).astype(BF16)
    for sq in range(u.shape[0] // t):
        rows = slice(sq * t, (sq + 1) * t)
        ab = jnp.concatenate([a[rows], b[rows]], axis=0)
        o_ref[rows, :] = _dot(ct_ref[...], ab).astype(BF16)


def _fft(proj, cw, sw, ct, *, t):
    R = proj.shape[1]
    W = FFT_WIDTH
    rows = t * _seqs_per_step(t)
    return pl.pallas_call(
        functools.partial(_fft_kernel, t=t),
        grid=(R // rows,),
        in_specs=[
            pl.BlockSpec((None, rows, W), lambda b: (1, b, 1)),
            pl.BlockSpec((W, W), lambda b: (0, 0)),
            pl.BlockSpec((W, W), lambda b: (0, 0)),
            pl.BlockSpec((t, 2 * t), lambda b: (0, 0)),
        ],
        out_specs=pl.BlockSpec((rows, W), lambda b: (b, 0)),
        out_shape=jax.ShapeDtypeStruct((R, W), BF16),
        compiler_params=_cparams(1),
        name="fourier_mix",
    )(proj, cw, sw, ct)


def _mix_kernel(x_ref, om_ref, og_ref, of_ref, g0_ref, g1_ref, g2_ref, mod_ref, n2_ref,
                wm_ref, wg_ref, wf_ref, wo_ref, rw_ref, x1_ref, h2_ref, aff_ref, *, seqs_per_tile):
    ym = _dot(om_ref[...], wm_ref[...])
    yg = _dot(og_ref[...], wg_ref[...])
    yf = _dot(of_ref[...], wf_ref[...])
    merged = g0_ref[...].astype(F32) * ym + g1_ref[...].astype(F32) * yg + g2_ref[...].astype(F32) * yf
    z = _dot(merged.astype(BF16), wo_ref[...])
    x1 = x_ref[...] + mod_ref[2] * z
    x1_ref[...] = x1
    h2 = _rms(x1, n2_ref[...]) * (1.0 + mod_ref[4]) + mod_ref[3]
    h2_ref[...] = h2.astype(BF16)
    h_hi = h2.astype(BF16)
    h_lo = (h2 - h_hi.astype(F32)).astype(BF16)
    lg = _dot(h_hi, rw_ref[...])
    lg = lg[:, :LANE] + lg[:, LANE:] + _dot(h_lo, rw_ref[:, :LANE])
    logits = lg.T[:N_EXPERTS]
    m = jnp.max(logits, axis=0, keepdims=True)
    p = jnp.exp(logits - m)
    aff = p / jnp.sum(p, axis=0, keepdims=True)
    if seqs_per_tile == 0:
        aff_ref[...] = aff
    else:
        t = ROW_TILE // seqs_per_tile
        for s in range(seqs_per_tile):
            aff_ref[s] = aff[:, s * t:(s + 1) * t]


def _mix(x, o_mla, o_gla, o_fft, proj, mods, n2, wm, wg, wf, wo, rwt, l, *, t, mod_row0, rows_per_mod):
    R, D = x.shape
    HW = H_MLA * D_V
    nseq = R // t

    def wspec(k):
        return pl.BlockSpec((None, k, D), lambda i: (l, 0, 0))

    if t <= ROW_TILE:
        seqs_per_tile = ROW_TILE // t
        aff_spec = pl.BlockSpec((seqs_per_tile, N_EXPERTS, t), lambda i: (i, 0, 0))
    else:
        seqs_per_tile = 0
        tiles = t // ROW_TILE
        aff_spec = pl.BlockSpec((None, N_EXPERTS, ROW_TILE), lambda i: (i // tiles, 0, i % tiles))
    return pl.pallas_call(
        functools.partial(_mix_kernel, seqs_per_tile=seqs_per_tile),
        grid=(R // ROW_TILE,),
        in_specs=[
            pl.BlockSpec((ROW_TILE, D), lambda i: (i, 0)),
            pl.BlockSpec((ROW_TILE, HW), lambda i: (i, 0)),
            pl.BlockSpec((ROW_TILE, HW), lambda i: (i, 0)),
            pl.BlockSpec((ROW_TILE, HW), lambda i: (i, 0)),
            pl.BlockSpec((None, ROW_TILE, D), lambda i: (4, i, 0)),
            pl.BlockSpec((None, ROW_TILE, D), lambda i: (5, i, 0)),
            pl.BlockSpec((None, ROW_TILE, D), lambda i: (6, i, 0)),
            _mod_spec(l, mod_row0, rows_per_mod),
            pl.BlockSpec((None, 1, D), lambda i: (l, 0, 0)),
            wspec(HW), wspec(HW), wspec(HW), wspec(D),
            pl.BlockSpec((None, D, 2 * LANE), lambda i: (l, 0, 0)),
        ],
        out_specs=[
            pl.BlockSpec((ROW_TILE, D), lambda i: (i, 0)),
            pl.BlockSpec((ROW_TILE, D), lambda i: (i, 0)),
            aff_spec,
        ],
        out_shape=[
            jax.ShapeDtypeStruct((R, D), F32),
            jax.ShapeDtypeStruct((R, D), BF16),
            jax.ShapeDtypeStruct((nseq, N_EXPERTS, t), F32),
        ],
        compiler_params=_cparams(1),
        name="mix_out",
    )(x, o_mla, o_gla, o_fft, proj, proj, proj, mods, n2, wm, wg, wf, wo, rwt)


def _route_kernel(a_ref, slot_ref, *, cap):
    a = a_ref[...]
    rows, t = a.shape

    def count_ge(x):
        return jnp.sum(jnp.where(a >= x, 1.0, 0.0), axis=1, keepdims=True)

    thr = jnp.zeros((rows, 1), jnp.int32)
    for bit in range(30, -1, -1):
        cand = thr | (1 << bit)
        thr = jnp.where(count_ge(pltpu.bitcast(cand, F32)) >= cap, cand, thr)
    lo = pltpu.bitcast(thr, F32)
    hi = pltpu.bitcast(thr + 1, F32)
    for _ in range(ROUTE_REFINE_STEPS):
        mid = lo + (hi - lo) * 0.5
        keep = count_ge(mid) >= cap
        lo = jnp.where(keep, mid, lo)
        hi = jnp.where(keep, hi, mid)
    gt = a > lo
    eq = a == lo
    need = cap - jnp.sum(jnp.where(gt, 1.0, 0.0), axis=1, keepdims=True)
    ri = lax.broadcasted_iota(jnp.int32, (t, t), 0)
    ci = lax.broadcasted_iota(jnp.int32, (t, t), 1)
    before = jnp.where(ri < ci, 1.0, 0.0).astype(BF16)
    eq_rank = _dot(jnp.where(eq, 1.0, 0.0).astype(BF16), before)
    sel = gt | (eq & (eq_rank < need))
    slot = _dot(jnp.where(sel, 1.0, 0.0).astype(BF16), before)
    slot_ref[...] = jnp.where(sel, slot, -1.0)


def _route(a, cap):
    return pl.pallas_call(
        functools.partial(_route_kernel, cap=cap),
        out_shape=jax.ShapeDtypeStruct(a.shape, F32),
        compiler_params=pltpu.CompilerParams(vmem_limit_bytes=VMEM_LIMIT),
        name="route",
    )(a)


def _gather_kernel(h_ref, slot_ref, aff_ref, xg_ref, g_ref, p_ref, *, cap):
    seqs, _, t = slot_ref.shape
    ci = lax.broadcasted_iota(jnp.int32, (cap, t), 0).astype(F32)
    for sq in range(seqs):
        slots = slice(sq * cap, (sq + 1) * cap)
        for e in range(N_EXPERTS):
            hit = ci == slot_ref[sq, e:e + 1, :]
            p_ref[sq, e * cap:(e + 1) * cap, :] = jnp.where(hit, 1.0, 0.0).astype(BF16)
            g_ref[e, slots, :] = jnp.sum(jnp.where(hit, aff_ref[sq, e:e + 1, :], 0.0), axis=1, keepdims=True)
        xg = _dot(p_ref[sq], h_ref[sq * t:(sq + 1) * t, :])
        for e in range(N_EXPERTS):
            xg_ref[e, slots, :] = xg[e * cap:(e + 1) * cap].astype(BF16)


def _gather(h2, slot, aff, *, t, cap):
    R, D = h2.shape
    nseq = R // t
    seqs = _seqs_per_step(t)
    return pl.pallas_call(
        functools.partial(_gather_kernel, cap=cap),
        grid=(nseq // seqs,),
        in_specs=[
            pl.BlockSpec((seqs * t, D), lambda b: (b, 0)),
            pl.BlockSpec((seqs, N_EXPERTS, t), lambda b: (b, 0, 0)),
            pl.BlockSpec((seqs, N_EXPERTS, t), lambda b: (b, 0, 0)),
        ],
        out_specs=[
            pl.BlockSpec((N_EXPERTS, seqs * cap, D), lambda b: (0, b, 0)),
            pl.BlockSpec((N_EXPERTS, seqs * cap, 1), lambda b: (0, b, 0)),
        ],
        out_shape=[
            jax.ShapeDtypeStruct((N_EXPERTS, nseq * cap, D), BF16),
            jax.ShapeDtypeStruct((N_EXPERTS, nseq * cap, 1), F32),
        ],
        scratch_shapes=[pltpu.VMEM((seqs, N_EXPERTS * cap, t), BF16)],
        compiler_params=_cparams(1),
        name="moe_gather",
    )(h2, slot, aff)


def _expert_kernel(xa_ref, xb_ref, ga_ref, gb_ref, w1_ref, w3_ref, w2_ref, ya_ref, yb_ref, acc_ref):
    f = pl.program_id(1)
    nf = pl.num_programs(1)
    na = xa_ref.shape[0]
    groups = ((xa_ref, ga_ref, ya_ref, 0), (xb_ref, gb_ref, yb_ref, na))

    def body(first, last):
        w1 = w1_ref[...].astype(BF16)
        w3 = w3_ref[...].astype(BF16)
        w2 = w2_ref[...].astype(BF16)
        for x_ref, g_ref, y_ref, base in groups:
            n = x_ref.shape[0]
            sub = min(n, EXPERT_ROW_TILE)
            for r in range(n // sub):
                rows = slice(r * sub, (r + 1) * sub)
                arows = slice(base + r * sub, base + (r + 1) * sub)
                x = x_ref[rows, :]
                a = _dot(x, w1)
                gt = _dot(x, w3)
                hm = (a * _sigmoid(a) * gt).astype(BF16)
                contrib = _dot(hm, w2)
                if not first:
                    contrib = acc_ref[arows, :] + contrib
                if last:
                    y_ref[rows, :] = (contrib * g_ref[rows, :]).astype(BF16)
                else:
                    acc_ref[arows, :] = contrib

    @pl.when(f == 0)
    def _():
        body(True, False)

    @pl.when((f > 0) & (f < nf - 1))
    def _():
        body(False, False)

    @pl.when(f == nf - 1)
    def _():
        body(False, True)


def _experts(xa, xb, ga, gb, w1, w3, w2, l):
    E, na, D = xa.shape
    nb = xb.shape[1]
    FT = EXPERT_F_TILE
    assert D_EXPERT // FT >= 2

    def rows_spec(n, w):
        return pl.BlockSpec((None, n, w), lambda e, f: (e, 0, 0))

    return pl.pallas_call(
        _expert_kernel,
        grid=(E, D_EXPERT // FT),
        in_specs=[
            rows_spec(na, D), rows_spec(nb, D), rows_spec(na, 1), rows_spec(nb, 1),
            pl.BlockSpec((None, None, D, FT), lambda e, f: (l, e, 0, f)),
            pl.BlockSpec((None, None, D, FT), lambda e, f: (l, e, 0, f)),
            pl.BlockSpec((None, None, FT, D), lambda e, f: (l, e, f, 0)),
        ],
        out_specs=[rows_spec(na, D), rows_spec(nb, D)],
        out_shape=[jax.ShapeDtypeStruct((E, na, D), BF16), jax.ShapeDtypeStruct((E, nb, D), BF16)],
        scratch_shapes=[pltpu.VMEM((na + nb, D), F32)],
        compiler_params=_cparams(2),
        name="moe_experts",
    )(xa, xb, ga, gb, w1, w3, w2)


def _scatter_kernel(x_ref, y_ref, slot_ref, mod_ref, o_ref, *, cap):
    seqs, _, t = slot_ref.shape
    n = N_EXPERTS * cap
    er = lax.broadcasted_iota(jnp.int32, (N_EXPERTS, n), 0)
    ec = lax.broadcasted_iota(jnp.int32, (N_EXPERTS, n), 1)
    expand = jnp.where(ec // cap == er, 1.0, 0.0).astype(BF16)
    want = (lax.broadcasted_iota(jnp.int32, (t, n), 1) % cap).astype(F32)
    for sq in range(seqs):
        rows = slice(sq * t, (sq + 1) * t)
        slot_wide = _dot_tn(slot_ref[sq].astype(BF16), expand)
        pt = jnp.where(slot_wide == want, 1.0, 0.0).astype(BF16)
        y = jnp.concatenate([y_ref[e, sq * cap:(sq + 1) * cap, :] for e in range(N_EXPERTS)], axis=0)
        o_ref[rows, :] = x_ref[rows, :] + mod_ref[5] * _dot(pt, y)


def _scatter(x1, y, slot, mods, l, *, t, cap, mod_row0, rows_per_mod):
    R, D = x1.shape
    seqs = _seqs_per_step(t)
    assert seqs == 1 or rows_per_mod is None
    return pl.pallas_call(
        functools.partial(_scatter_kernel, cap=cap),
        grid=(R // (seqs * t),),
        in_specs=[
            pl.BlockSpec((seqs * t, D), lambda b: (b, 0)),
            pl.BlockSpec((N_EXPERTS, seqs * cap, D), lambda b: (0, b, 0)),
            pl.BlockSpec((seqs, N_EXPERTS, t), lambda b: (b, 0, 0)),
            _mod_spec(l, mod_row0, rows_per_mod),
        ],
        out_specs=pl.BlockSpec((seqs * t, D), lambda b: (b, 0)),
        out_shape=jax.ShapeDtypeStruct((R, D), F32),
        compiler_params=_cparams(1),
        name="moe_scatter",
    )(x1, y, slot, mods)


def _final_kernel(x_ref, g_ref, o_ref):
    o_ref[...] = _rms(x_ref[...], g_ref[...])


def _final_norm(x, g):
    R, D = x.shape
    return pl.pallas_call(
        _final_kernel,
        grid=(R // ROW_TILE,),
        in_specs=[pl.BlockSpec((ROW_TILE, D), lambda i: (i, 0)), pl.BlockSpec((1, D), lambda i: (0, 0))],
        out_specs=pl.BlockSpec((ROW_TILE, D), lambda i: (i, 0)),
        out_shape=jax.ShapeDtypeStruct((R, D), F32),
        compiler_params=_cparams(1),
        name="final_norm",
    )(x, g)


def _rope_tables(t_lat):
    half = D_ROPE // 2
    nfreq = half // 2
    pos = np.arange(t_lat)
    inv = 1.0 / (ROPE_BASE ** (np.arange(0, half, 2, dtype=np.float64) / half))
    cos = np.ones((t_lat, HEAD_PAD))
    s_up = np.zeros((t_lat, HEAD_PAD))
    s_dn = np.zeros((t_lat, HEAD_PAD))
    for part, p in enumerate((pos // GRID_W, pos % GRID_W)):
        ang = p[:, None].astype(np.float64) * inv
        o = KR_OFF + part * half
        cos[:, o:o + nfreq] = np.cos(ang)
        cos[:, o + nfreq:o + half] = np.cos(ang)
        s_up[:, o:o + nfreq] = -np.sin(ang)
        s_dn[:, o + nfreq:o + half] = np.sin(ang)
    tab = np.stack([cos, s_up, s_dn], axis=0).reshape(3, t_lat // ROW_TILE, ROW_TILE, HEAD_PAD)
    return jnp.asarray(np.moveaxis(tab, 1, 0), dtype=F32)


def _dft_mats(n, scale):
    k = np.arange(n)
    ang = (2.0 * np.pi / n) * ((k[:, None] * k[None, :]) % n)
    return np.cos(ang) * scale, np.sin(ang) * scale


def _dft_tables(t):
    cw, sw = _dft_mats(FFT_GROUP_W, FFT_GROUP_W ** -0.5)
    eye = np.eye(FFT_GROUPS)
    ct, st = _dft_mats(t, t ** -0.5)

    def const(a):
        return jnp.asarray(a, dtype=F32).astype(BF16)

    return const(np.kron(eye, cw)), const(np.kron(eye, sw)), const(np.concatenate([ct, -st], axis=1))


def _layout_w_in(w_in):
    L, D, _ = w_in.shape
    sizes = (H_MLA * (D_NOPE + D_ROPE), KV_RANK, D_ROPE, H_GLA * GLA_DK, H_GLA * GLA_DK, H_GLA * GLA_DV,
             H_GLA * GLA_DV, 2 * GLA_GATE_RANK, FFT_WIDTH, 3 * D_MODEL)
    offs = np.concatenate([[0], np.cumsum(sizes)])
    seg = [w_in[:, :, offs[i]:offs[i + 1]] for i in range(len(sizes))]
    q, ckv, kr, gq, gk, gv, gr, glr, uf, mg = seg
    q = q.reshape(L, D, H_MLA, D_NOPE + D_ROPE)
    q = jnp.pad(q, ((0, 0), (0, 0), (0, 0), (0, HEAD_PAD - D_NOPE - D_ROPE))).reshape(L, D, H_MLA * HEAD_PAD)
    kr = jnp.pad(kr, ((0, 0), (0, 0), (KR_OFF, HEAD_PAD - KR_OFF - D_ROPE)))
    glr = jnp.pad(glr, ((0, 0), (0, 0), (0, LANE - 2 * GLA_GATE_RANK)))
    cols = jnp.concatenate([q, ckv, kr, glr, uf, gq, gk, gv, gr, mg], axis=-1).astype(BF16)
    return jnp.moveaxis(cols.reshape(L, D, N_CHUNK, D), 2, 1)


def _layout_w_ukv(w_ukv):
    L = w_ukv.shape[0]
    w = w_ukv.reshape(L, KV_RANK, H_MLA, D_NOPE + D_V)
    k = jnp.pad(w[..., :D_NOPE], ((0, 0), (0, 0), (0, 0), (0, HEAD_PAD - D_NOPE)))
    v = w[..., D_NOPE:].reshape(L, KV_RANK, H_MLA // 2, 2, D_V)
    z = jnp.zeros_like(v[:, :, :, 0])
    v_even = jnp.concatenate([v[:, :, :, 0], z], axis=-1)
    v_odd = jnp.concatenate([z, v[:, :, :, 1]], axis=-1)
    v2 = jnp.stack([v_even, v_odd], axis=3)
    return (k.reshape(L, KV_RANK, H_MLA * HEAD_PAD).astype(BF16),
            v2.reshape(L, KV_RANK, H_MLA * HEAD_PAD).astype(BF16))


def _layout_gate(gate_w, gate_b):
    L = gate_w.shape[0]
    HW = H_GLA * GLA_DK
    wg = jnp.zeros((L, LANE, 2 * HW), F32)
    for d in range(2):
        wg = wg.at[:, d * GLA_GATE_RANK:(d + 1) * GLA_GATE_RANK, d * HW:(d + 1) * HW].set(gate_w[:, d])
    return wg.astype(BF16), gate_b.reshape(L, 1, 2 * HW)


def kernel(x_prompt, x_sample, cache_ckv, cache_krope, state_gla, c, c_ctx, ada_w, ada_b, norm1_g, norm2_g, w_in, mla_kv_norm_g, mla_w_ukv, gla_gate_w, gla_gate_b, gla_norm_g, w_mla_out, w_gla_out, w_fft_out, w_o, router_w, exp_w1, exp_w3, exp_w2, final_norm_g):
    nb_c, t_c, D = x_prompt.shape
    nb_l, t_l, _ = x_sample.shape
    assert (nb_c * t_c) % ROW_TILE == 0 and ROW_TILE % t_c == 0 and t_l % ROW_TILE == 0 and 1 + nb_l <= MOD_ROWS
    cap_c = CAPACITY_FACTOR * t_c // N_EXPERTS
    cap_l = CAPACITY_FACTOR * t_l // N_EXPERTS

    c_all = jnp.zeros((MOD_ROWS, D), F32).at[0].set(c_ctx).at[1:1 + nb_l].set(c)
    mods = _mods(c_all, ada_w, ada_b).reshape(DEPTH, N_MOD, MOD_ROWS, 1, D)
    rope_tab = _rope_tables(t_l)
    dft_c = _dft_tables(t_c)
    dft_l = _dft_tables(t_l)
    w_in_r = _layout_w_in(w_in)
    wuk, wuv = _layout_w_ukv(mla_w_ukv)
    wgate, bgate = _layout_gate(gla_gate_w, gla_gate_b)
    n1 = norm1_g.reshape(DEPTH, 1, D)
    n2 = norm2_g.reshape(DEPTH, 1, D)
    kvg = mla_kv_norm_g.reshape(DEPTH, 1, KV_RANK)
    glg = gla_norm_g.reshape(DEPTH, 1, GLA_DV)
    wm = w_mla_out.astype(BF16)
    wgo = w_gla_out.astype(BF16)
    wfo = w_fft_out.astype(BF16)
    wo = w_o.astype(BF16)
    rw_hi = router_w.astype(BF16)
    rw_lo = (router_w - rw_hi.astype(F32)).astype(BF16)
    rwt = jnp.zeros((DEPTH, D, 2 * LANE), BF16).at[:, :, :N_EXPERTS].set(rw_hi)
    rwt = rwt.at[:, :, LANE:LANE + N_EXPERTS].set(rw_lo)
    ckr_pad = jnp.pad(cache_krope, ((0, 0), (0, 0), (0, 0), (KR_OFF, HEAD_PAD - KR_OFF - D_ROPE)))

    ctx = dict(t=t_c, cap=cap_c, mod_row0=0, rows_per_mod=None)
    lat = dict(t=t_l, cap=cap_l, mod_row0=1, rows_per_mod=t_l // ROW_TILE)
    xs = [x_prompt.reshape(nb_c * t_c, D), x_sample.reshape(nb_l * t_l, D)]
    ckv_out, kr_out, st_out = [], [], []

    def pre_experts(x, grp, l, is_ctx):
        t, cap = grp["t"], grp["cap"]
        mod_kw = dict(mod_row0=grp["mod_row0"], rows_per_mod=grp["rows_per_mod"])
        proj, ckv, krp = _inproj(x, mods, n1, kvg, w_in_r, l, rope_tab=None if is_ctx else rope_tab, **mod_kw)
        if is_ctx:
            ckv_out.append(ckv.reshape(nb_c, t_c, KV_RANK))
            kr_out.append(krp[:, KR_OFF:KR_OFF + D_ROPE].reshape(nb_c, t_c, D_ROPE))
            o_mla = _attn(proj, ckv, krp, wuk, wuv, l, t_q=t)
            o_gla, st = _gla(proj, wgate, bgate, glg, l, t=t, want_state=True)
            st_out.append(st)
            o_fft = _fft(proj, *dft_c, t=t)
        else:
            o_mla = _attn(proj, ckv, krp, wuk, wuv, l, t_q=t, cache=(cache_ckv, ckr_pad))
            (o_gla,) = _gla(proj, wgate, bgate, glg, l, t=t, state=state_gla)
            o_fft = _fft(proj, *dft_l, t=t)
        rows_per_mod_mix = grp["rows_per_mod"]
        x1, h2, aff = _mix(x, o_mla, o_gla, o_fft, proj, mods, n2, wm, wgo, wfo, wo, rwt, l, t=t,
                           mod_row0=grp["mod_row0"], rows_per_mod=rows_per_mod_mix)
        nseq = aff.shape[0]
        slot = _route(aff.reshape(nseq * N_EXPERTS, t), cap).reshape(nseq, N_EXPERTS, t)
        xg, g = _gather(h2, slot, aff, t=t, cap=cap)
        return x1, slot, xg, g

    for l in range(DEPTH):
        x1_c, slot_c, xg_c, g_c = pre_experts(xs[0], ctx, l, True)
        x1_l, slot_l, xg_l, g_l = pre_experts(xs[1], lat, l, False)
        y_c, y_l = _experts(xg_c, xg_l, g_c, g_l, exp_w1, exp_w3, exp_w2, l)
        xs = [
            _scatter(x1_c, y_c, slot_c, mods, l, t=t_c, cap=cap_c, mod_row0=0, rows_per_mod=None),
            _scatter(x1_l, y_l, slot_l, mods, l, t=t_l, cap=cap_l, mod_row0=1, rows_per_mod=1),
        ]

    fg = final_norm_g.reshape(1, D)
    y_prompt = _final_norm(xs[0], fg).reshape(nb_c, t_c, D)
    y_sample = _final_norm(xs[1], fg).reshape(nb_l, t_l, D)
    return (y_prompt, y_sample, jnp.stack(ckv_out, axis=1), jnp.stack(kr_out, axis=1),
            jnp.stack(st_out, axis=1))
```

```python
import functools
import math

import jax
import jax.numpy as jnp
import numpy as np
from jax import lax
from jax.experimental import pallas as pl
from jax.experimental.pallas import tpu as pltpu

F32 = jnp.float32
BF16 = jnp.bfloat16

D_MODEL = 1024
DEPTH = 4
GRID_W = 64
H_MLA = 8
D_NOPE = 64
D_ROPE = 32
D_V = 64
KV_RANK = 256
ROPE_BASE = 10000.0
MLA_SCALE = (D_NOPE + D_ROPE) ** -0.5
H_GLA = 4
GLA_DK = 128
GLA_DV = 128
GLA_GATE_RANK = 16
GLA_TAU = 16.0
GLA_CHUNK = 64
FFT_GROUPS = 4
FFT_GROUP_W = 128
FFT_WIDTH = FFT_GROUPS * FFT_GROUP_W
N_EXPERTS = 16
D_EXPERT = 2048
CAPACITY_FACTOR = 2
N_MOD = 6
EPS = 1e-6

LANE = 128
ROW_TILE = 512
HEAD_PAD = 128
KR_OFF = D_NOPE
N_CHUNK = 7
MOD_ROWS = 16
MOD_PER_STEP = 2
EXPERT_F_TILE = 512
EXPERT_ROW_TILE = 512
ATTN_Q_TILE = 512
SHORT_SEQ_ROWS = 1024
ROUTE_REFINE_STEPS = 16
Q_PRESCALE = MLA_SCALE * math.log2(math.e)
VMEM_LIMIT = 56 * 1024 * 1024


def _cparams(n_axes):
    return pltpu.CompilerParams(dimension_semantics=("arbitrary",) * n_axes, vmem_limit_bytes=VMEM_LIMIT)


def _dot(a, b):
    return jnp.dot(a, b, preferred_element_type=F32)


def _dot_nt(a, b):
    return lax.dot_general(a, b, (((1,), (1,)), ((), ())), preferred_element_type=F32)


def _dot_tn(a, b):
    return lax.dot_general(a, b, (((0,), (0,)), ((), ())), preferred_element_type=F32)


def _sigmoid(x):
    return 1.0 / (1.0 + jnp.exp(-x))


def _rms(x, g):
    return x * lax.rsqrt(jnp.mean(x * x, axis=-1, keepdims=True) + EPS) * g


def _mods_kernel(c_ref, w_ref, b_ref, o_ref):
    c = c_ref[...]
    s = (c * _sigmoid(c)).astype(BF16)
    m = _dot(s, w_ref[...].astype(BF16))
    for k in range(MOD_PER_STEP):
        o_ref[k] = m[:, k * D_MODEL:(k + 1) * D_MODEL] + b_ref[k]


def _mods(c_all, ada_w, ada_b):
    D = D_MODEL
    return pl.pallas_call(
        _mods_kernel,
        grid=(DEPTH, N_MOD // MOD_PER_STEP),
        in_specs=[
            pl.BlockSpec((MOD_ROWS, D), lambda l, k: (0, 0)),
            pl.BlockSpec((None, D, MOD_PER_STEP * D), lambda l, k: (l, 0, k)),
            pl.BlockSpec((None, MOD_PER_STEP, 1, D), lambda l, k: (l, k, 0, 0)),
        ],
        out_specs=pl.BlockSpec((None, MOD_PER_STEP, MOD_ROWS, D), lambda l, k: (l, k, 0, 0)),
        out_shape=jax.ShapeDtypeStruct((DEPTH, N_MOD, MOD_ROWS, D), F32),
        compiler_params=_cparams(2),
        name="adaln_mods",
    )(c_all, ada_w, ada_b.reshape(DEPTH, N_MOD, 1, D))


def _mod_spec(l, mod_row0, rows_per_mod):
    if rows_per_mod is None:
        return pl.BlockSpec((None, N_MOD, None, 1, D_MODEL), lambda i: (l, 0, mod_row0, 0, 0))
    return pl.BlockSpec((None, N_MOD, None, 1, D_MODEL), lambda i: (l, 0, mod_row0 + i // rows_per_mod, 0, 0))


def _inproj_kernel(*refs, has_rope):
    if has_rope:
        x_ref, mod_ref, n1_ref, kvg_ref, w_ref, rope_ref, proj_ref, ckv_ref, krp_ref = refs
        cos, sin_up, sin_dn = rope_ref[0], rope_ref[1], rope_ref[2]

        def rope(a):
            return a * cos + pltpu.roll(a, LANE - 8, 1) * sin_up + pltpu.roll(a, 8, 1) * sin_dn
    else:
        x_ref, mod_ref, n1_ref, kvg_ref, w_ref, proj_ref, ckv_ref, krp_ref = refs

        def rope(a):
            return a

    x = x_ref[...]
    h = _rms(x, n1_ref[...]) * (1.0 + mod_ref[1]) + mod_ref[0]
    hb = h.astype(BF16)

    acc = _dot(hb, w_ref[0])
    for hd in range(H_MLA):
        sl = slice(hd * HEAD_PAD, (hd + 1) * HEAD_PAD)
        proj_ref[0, :, sl] = (rope(acc[:, sl]) * Q_PRESCALE).astype(BF16)

    acc = _dot(hb, w_ref[1])
    ckv_ref[...] = _rms(acc[:, :KV_RANK], kvg_ref[...])
    krp_ref[...] = rope(acc[:, KV_RANK:KV_RANK + HEAD_PAD])
    proj_ref[1] = acc.astype(BF16)

    for c in (2, 3):
        proj_ref[c] = _dot(hb, w_ref[c]).astype(BF16)
    for c in (4, 5, 6):
        proj_ref[c] = _sigmoid(_dot(hb, w_ref[c])).astype(BF16)


def _inproj(x, mods, n1, kvg, w_in_r, l, *, mod_row0, rows_per_mod, rope_tab=None):
    R, D = x.shape
    in_specs = [
        pl.BlockSpec((ROW_TILE, D), lambda i: (i, 0)),
        _mod_spec(l, mod_row0, rows_per_mod),
        pl.BlockSpec((None, 1, D), lambda i: (l, 0, 0)),
        pl.BlockSpec((None, 1, KV_RANK), lambda i: (l, 0, 0)),
        pl.BlockSpec((None, N_CHUNK, D, D), lambda i: (l, 0, 0, 0), pipeline_mode=pl.Buffered(1)),
    ]
    args = [x, mods, n1, kvg, w_in_r]
    if rope_tab is not None:
        nper = rope_tab.shape[0]
        in_specs.append(pl.BlockSpec((None, 3, ROW_TILE, HEAD_PAD), lambda i: (i % nper, 0, 0, 0)))
        args.append(rope_tab)
    return pl.pallas_call(
        functools.partial(_inproj_kernel, has_rope=rope_tab is not None),
        grid=(R // ROW_TILE,),
        in_specs=in_specs,
        out_specs=[
            pl.BlockSpec((N_CHUNK, ROW_TILE, D), lambda i: (0, i, 0)),
            pl.BlockSpec((ROW_TILE, KV_RANK), lambda i: (i, 0)),
            pl.BlockSpec((ROW_TILE, HEAD_PAD), lambda i: (i, 0)),
        ],
        out_shape=[
            jax.ShapeDtypeStruct((N_CHUNK, R, D), BF16),
            jax.ShapeDtypeStruct((R, KV_RANK), F32),
            jax.ShapeDtypeStruct((R, HEAD_PAD), F32),
        ],
        compiler_params=_cparams(1),
        name="inproj",
    )(*args)


def _attn_kernel(*refs, t_q, seqs, has_cache, q_blk):
    if has_cache:
        q_ref, ckv_ref, krp_ref, cckv_ref, ckr_ref, wuk_ref, wuv_ref, o_ref = refs
        ckv = jnp.concatenate([cckv_ref[...], ckv_ref[...]], axis=0)
        krp = jnp.concatenate([ckr_ref[...], krp_ref[...]], axis=0)
    else:
        q_ref, ckv_ref, krp_ref, wuk_ref, wuv_ref, o_ref = refs
        ckv = ckv_ref[...]
        krp = krp_ref[...]
    t_k = ckv.shape[0] // seqs
    ckvb = ckv.astype(BF16)
    k_all = _dot(ckvb, wuk_ref[...])
    v_all = _dot(ckvb, wuv_ref[...]).astype(BF16)
    for sq in range(seqs):
        keys = slice(sq * t_k, (sq + 1) * t_k)
        for pair in range(H_MLA // 2):
            k_pair = [(k_all[keys, hd * HEAD_PAD:(hd + 1) * HEAD_PAD] + krp[keys]).astype(BF16)
                      for hd in (2 * pair, 2 * pair + 1)]
            for qb in range(t_q // q_blk):
                rows = slice(sq * t_q + qb * q_blk, sq * t_q + (qb + 1) * q_blk)
                o_pair = None
                for j, hd in enumerate((2 * pair, 2 * pair + 1)):
                    sl = slice(hd * HEAD_PAD, (hd + 1) * HEAD_PAD)
                    s = _dot_nt(q_ref[rows, sl], k_pair[j])
                    p = jnp.exp2(s - jnp.max(s, axis=-1, keepdims=True))
                    den = jnp.sum(p, axis=-1, keepdims=True)
                    o_h = _dot(p.astype(BF16), v_all[keys, sl]) / den
                    o_pair = o_h if o_pair is None else o_pair + o_h
                o_ref[rows, pair * LANE:(pair + 1) * LANE] = o_pair.astype(BF16)


def _seqs_per_step(t, has_cache=False):
    return 1 if has_cache else max(1, SHORT_SEQ_ROWS // t)


def _attn(proj, ckv, krp, wuk, wuv, l, *, t_q, cache=None):
    R = ckv.shape[0]
    seqs = _seqs_per_step(t_q, cache is not None)
    t_q_seq, t_q = t_q, t_q * seqs
    in_specs = [
        pl.BlockSpec((None, t_q, D_MODEL), lambda b: (0, b, 0)),
        pl.BlockSpec((t_q, KV_RANK), lambda b: (b, 0)),
        pl.BlockSpec((t_q, HEAD_PAD), lambda b: (b, 0)),
    ]
    args = [proj, ckv, krp]
    if cache is not None:
        cckv, ckr = cache
        past = cckv.shape[2]
        in_specs += [
            pl.BlockSpec((None, None, past, KV_RANK), lambda b: (b, l, 0, 0)),
            pl.BlockSpec((None, None, past, HEAD_PAD), lambda b: (b, l, 0, 0)),
        ]
        args += [cckv, ckr]
    in_specs += [
        pl.BlockSpec((None, KV_RANK, H_MLA * HEAD_PAD), lambda b: (l, 0, 0)),
        pl.BlockSpec((None, KV_RANK, H_MLA * HEAD_PAD), lambda b: (l, 0, 0)),
    ]
    args += [wuk, wuv]
    return pl.pallas_call(
        functools.partial(_attn_kernel, t_q=t_q_seq, seqs=seqs, has_cache=cache is not None,
                          q_blk=min(t_q_seq, ATTN_Q_TILE)),
        grid=(R // t_q,),
        in_specs=in_specs,
        out_specs=pl.BlockSpec((t_q, H_MLA * D_V), lambda b: (b, 0)),
        out_shape=jax.ShapeDtypeStruct((R, H_MLA * D_V), BF16),
        compiler_params=_cparams(1),
        name="mla_attn",
    )(*args)


def _gla_kernel(*refs, t, seqs, has_state, want_state):
    refs = list(refs)
    q_ref, k_ref, v_ref, r_ref, glr_ref, wg_ref, bg_ref, g_ref = refs[:8]
    pos = 8
    s0_ref = None
    if has_state:
        s0_ref = refs[pos]
        pos += 1
    o_ref = refs[pos]
    pos += 1
    so_ref = None
    if want_state:
        so_ref = refs[pos]
        pos += 1
    b_ref, of_ref, ob_ref, st_ref = refs[pos:pos + 4]

    CH = GLA_CHUNK
    BLK = 256
    HW = H_GLA * GLA_DK

    pre = _dot(glr_ref[...], wg_ref[...]) + bg_ref[...]
    la = (jnp.minimum(pre, 0.0) - jnp.log(1.0 + jnp.exp(-jnp.abs(pre)))) * (1.0 / GLA_TAU)
    la_hi = la.astype(BF16)
    la_lo = (la - la_hi.astype(F32)).astype(BF16)
    ri = lax.broadcasted_iota(jnp.int32, (BLK, BLK), 0)
    ci = lax.broadcasted_iota(jnp.int32, (BLK, BLK), 1)
    same = (ri // CH) == (ci // CH)
    tri_f = jnp.where(same & (ci <= ri), 1.0, 0.0).astype(BF16)
    tri_b = jnp.where(same & (ci >= ri), 1.0, 0.0).astype(BF16)
    for blk in range(t // BLK):
        rows = slice(blk * BLK, (blk + 1) * BLK)
        hi, lo = la_hi[rows], la_lo[rows]
        b_ref[rows, :HW] = _dot(tri_f, hi[:, :HW]) + _dot(tri_f, lo[:, :HW])
        b_ref[rows, HW:] = _dot(tri_b, hi[:, HW:]) + _dot(tri_b, lo[:, HW:])

    n_st = 2 * H_GLA
    for sq in range(seqs):
        for h in range(H_GLA):
            for d in range(2):
                if has_state:
                    st_ref[sq * n_st + 2 * h + d] = s0_ref[d, h].T
                else:
                    st_ref[sq * n_st + 2 * h + d] = jnp.zeros((GLA_DV, GLA_DK), F32)

    masks = (same & (ci <= ri), same & (ci >= ri))
    scale = GLA_DK ** -0.5
    CPB = BLK // CH
    zero_chunk = jnp.zeros((CH, GLA_DK), BF16)

    def chunk_diag(x):
        return jnp.concatenate(
            [jnp.concatenate([x[c * CH:(c + 1) * CH] if j == c else zero_chunk for j in range(CPB)], axis=1)
             for c in range(CPB)], axis=0)

    def unit(sq, h, d, blk, vt):
        rows = slice(blk * BLK, (blk + 1) * BLK)
        cols = slice(h * GLA_DK, (h + 1) * GLA_DK)
        b = b_ref[rows, d * HW + h * GLA_DK:d * HW + (h + 1) * GLA_DK]
        last = [b[c * CH + CH - 1:c * CH + CH] if d == 0 else b[c * CH:c * CH + 1] for c in range(CPB)]
        b_last = jnp.concatenate([jnp.broadcast_to(x, (CH, GLA_DK)) for x in last], axis=0)
        q = q_ref[rows, cols].astype(F32) * scale
        k = k_ref[rows, cols].astype(F32)
        q_t = (q * jnp.exp(b)).astype(BF16)
        k_t = (k * jnp.exp(-b)).astype(BF16)
        k_end = (k * jnp.exp(b_last - b)).astype(BF16)
        att = jnp.where(masks[d], _dot_nt(q_t, k_t), 0.0).astype(BF16)
        ds = _dot(vt, chunk_diag(k_end))
        st = st_ref[sq * n_st + 2 * h + d]
        prev = [None] * CPB
        for c in (range(CPB) if d == 0 else range(CPB - 1, -1, -1)):
            prev[c] = st.astype(BF16)
            st = st * jnp.exp(last[c]) + ds[:, c * GLA_DK:(c + 1) * GLA_DK]
        st_ref[sq * n_st + 2 * h + d] = st
        o = _dot_nt(jnp.concatenate([att, chunk_diag(q_t)], axis=1),
                    jnp.concatenate([vt] + prev, axis=1))
        if d == 0:
            of_ref[rows, cols] = o
        else:
            ob_ref[rows, cols] = o

    n_blk = t // seqs // BLK
    for n in range(n_blk):
        for sq in range(seqs):
            for h in range(H_GLA):
                cols = slice(h * GLA_DV, (h + 1) * GLA_DV)
                for d, blk in ((0, sq * n_blk + n), (1, sq * n_blk + n_blk - 1 - n)):
                    vt = v_ref[blk * BLK:(blk + 1) * BLK, cols].astype(F32).T.astype(BF16)
                    unit(sq, h, d, blk, vt)

    if want_state:
        for sq in range(seqs):
            for h in range(H_GLA):
                for d in range(2):
                    so_ref[sq, d, h] = st_ref[sq * n_st + 2 * h + d].T

    g = g_ref[...]
    for h in range(H_GLA):
        cols = slice(h * GLA_DV, (h + 1) * GLA_DV)
        o = _rms(of_ref[:, cols] + ob_ref[:, cols], g)
        r = r_ref[:, cols].astype(F32)
        o_ref[:, cols] = (o * (r * _sigmoid(r))).astype(BF16)


def _gla(proj, glr_w, glr_b, gla_g, l, *, t, state=None, want_state=False):
    R = proj.shape[1]
    HW = H_GLA * GLA_DK
    nseq = R // t
    seqs = _seqs_per_step(t, state is not None)
    t = t * seqs

    def spec(chunk, half):
        return pl.BlockSpec((None, t, HW), lambda b: (chunk, b, half))

    in_specs = [
        spec(2, 0), spec(2, 1), spec(3, 0), spec(3, 1),
        pl.BlockSpec((None, t, LANE), lambda b: (1, b, 3)),
        pl.BlockSpec((None, LANE, 2 * HW), lambda b: (l, 0, 0)),
        pl.BlockSpec((None, 1, 2 * HW), lambda b: (l, 0, 0)),
        pl.BlockSpec((None, 1, GLA_DV), lambda b: (l, 0, 0)),
    ]
    args = [proj, proj, proj, proj, proj, glr_w, glr_b, gla_g]
    if state is not None:
        in_specs.append(pl.BlockSpec((None, None, 2, H_GLA, GLA_DK, GLA_DV), lambda b: (b, l, 0, 0, 0, 0)))
        args.append(state)
    out_specs = [pl.BlockSpec((t, HW), lambda b: (b, 0))]
    out_shape = [jax.ShapeDtypeStruct((R, HW), BF16)]
    if want_state:
        out_specs.append(pl.BlockSpec((seqs, 2, H_GLA, GLA_DK, GLA_DV), lambda b: (b, 0, 0, 0, 0)))
        out_shape.append(jax.ShapeDtypeStruct((nseq, 2, H_GLA, GLA_DK, GLA_DV), F32))
    return pl.pallas_call(
        functools.partial(_gla_kernel, t=t, seqs=seqs, has_state=state is not None, want_state=want_state),
        grid=(nseq // seqs,),
        in_specs=in_specs,
        out_specs=out_specs,
        out_shape=out_shape,
        scratch_shapes=[pltpu.VMEM((t, 2 * HW), F32), pltpu.VMEM((t, HW), F32), pltpu.VMEM((t, HW), F32),
                        pltpu.VMEM((seqs * 2 * H_GLA, GLA_DV, GLA_DK), F32)],
        compiler_params=_cparams(1),
        name="gla",
    )(*args)


def _fft_kernel(u_ref, cw_ref, sw_ref, ct_ref, o_ref, *, t):
    u = u_ref[...]
    a = _dot(u, cw_ref[...]).astype(BF16)
    b = _dot(u, sw_ref[...]).astype(BF16)
    for sq in range(u.shape[0] // t):
        rows = slice(sq * t, (sq + 1) * t)
        ab = jnp.concatenate([a[rows], b[rows]], axis=0)
        o_ref[rows, :] = _dot(ct_ref[...], ab).astype(BF16)


def _fft(proj, cw, sw, ct, *, t):
    R = proj.shape[1]
    W = FFT_WIDTH
    rows = t * _seqs_per_step(t)
    return pl.pallas_call(
        functools.partial(_fft_kernel, t=t),
        grid=(R // rows,),
        in_specs=[
            pl.BlockSpec((None, rows, W), lambda b: (1, b, 1)),
            pl.BlockSpec((W, W), lambda b: (0, 0)),
            pl.BlockSpec((W, W), lambda b: (0, 0)),
            pl.BlockSpec((t, 2 * t), lambda b: (0, 0)),
        ],
        out_specs=pl.BlockSpec((rows, W), lambda b: (b, 0)),
        out_shape=jax.ShapeDtypeStruct((R, W), BF16),
        compiler_params=_cparams(1),
        name="fourier_mix",
    )(proj, cw, sw, ct)


def _mix_kernel(x_ref, om_ref, og_ref, of_ref, g0_ref, g1_ref, g2_ref, mod_ref, n2_ref,
                wm_ref, wg_ref, wf_ref, wo_ref, rw_ref, x1_ref, h2_ref, aff_ref, *, seqs_per_tile):
    ym = _dot(om_ref[...], wm_ref[...])
    yg = _dot(og_ref[...], wg_ref[...])
    yf = _dot(of_ref[...], wf_ref[...])
    merged = g0_ref[...].astype(F32) * ym + g1_ref[...].astype(F32) * yg + g2_ref[...].astype(F32) * yf
    z = _dot(merged.astype(BF16), wo_ref[...])
    x1 = x_ref[...] + mod_ref[2] * z
    x1_ref[...] = x1
    h2 = _rms(x1, n2_ref[...]) * (1.0 + mod_ref[4]) + mod_ref[3]
    h_hi = h2.astype(BF16)
    h2_ref[...] = h_hi
    h_lo = (h2 - h_hi.astype(F32)).astype(BF16)
    lg = _dot(h_hi, rw_ref[...])
    lg = lg[:, :LANE] + lg[:, LANE:] + _dot(h_lo, rw_ref[:, :LANE])
    logits = lg.T[:N_EXPERTS]
    m = jnp.max(logits, axis=0, keepdims=True)
    p = jnp.exp(logits - m)
    aff = p / jnp.sum(p, axis=0, keepdims=True)
    if seqs_per_tile == 0:
        aff_ref[...] = aff
    else:
        t = ROW_TILE // seqs_per_tile
        for s in range(seqs_per_tile):
            aff_ref[s] = aff[:, s * t:(s + 1) * t]


def _mix(x, o_mla, o_gla, o_fft, proj, mods, n2, wm, wg, wf, wo, rwt, l, *, t, mod_row0, rows_per_mod):
    R, D = x.shape
    HW = H_MLA * D_V
    nseq = R // t

    def wspec(k):
        return pl.BlockSpec((None, k, D), lambda i: (l, 0, 0))

    if t <= ROW_TILE:
        seqs_per_tile = ROW_TILE // t
        aff_spec = pl.BlockSpec((seqs_per_tile, N_EXPERTS, t), lambda i: (i, 0, 0))
    else:
        seqs_per_tile = 0
        tiles = t // ROW_TILE
        aff_spec = pl.BlockSpec((None, N_EXPERTS, ROW_TILE), lambda i: (i // tiles, 0, i % tiles))
    return pl.pallas_call(
        functools.partial(_mix_kernel, seqs_per_tile=seqs_per_tile),
        grid=(R // ROW_TILE,),
        in_specs=[
            pl.BlockSpec((ROW_TILE, D), lambda i: (i, 0)),
            pl.BlockSpec((ROW_TILE, HW), lambda i: (i, 0)),
            pl.BlockSpec((ROW_TILE, HW), lambda i: (i, 0)),
            pl.BlockSpec((ROW_TILE, HW), lambda i: (i, 0)),
            pl.BlockSpec((None, ROW_TILE, D), lambda i: (4, i, 0)),
            pl.BlockSpec((None, ROW_TILE, D), lambda i: (5, i, 0)),
            pl.BlockSpec((None, ROW_TILE, D), lambda i: (6, i, 0)),
            _mod_spec(l, mod_row0, rows_per_mod),
            pl.BlockSpec((None, 1, D), lambda i: (l, 0, 0)),
            wspec(HW), wspec(HW), wspec(HW), wspec(D),
            pl.BlockSpec((None, D, 2 * LANE), lambda i: (l, 0, 0)),
        ],
        out_specs=[
            pl.BlockSpec((ROW_TILE, D), lambda i: (i, 0)),
            pl.BlockSpec((ROW_TILE, D), lambda i: (i, 0)),
            aff_spec,
        ],
        out_shape=[
            jax.ShapeDtypeStruct((R, D), F32),
            jax.ShapeDtypeStruct((R, D), BF16),
            jax.ShapeDtypeStruct((nseq, N_EXPERTS, t), F32),
        ],
        compiler_params=_cparams(1),
        name="mix_out",
    )(x, o_mla, o_gla, o_fft, proj, proj, proj, mods, n2, wm, wg, wf, wo, rwt)


def _route_kernel(a_ref, slot_ref, *, cap):
    a = a_ref[...]
    rows, t = a.shape

    def count_ge(x):
        return jnp.sum(jnp.where(a >= x, 1.0, 0.0), axis=1, keepdims=True)

    thr = jnp.zeros((rows, 1), jnp.int32)
    for bit in range(30, -1, -1):
        cand = thr | (1 << bit)
        thr = jnp.where(count_ge(pltpu.bitcast(cand, F32)) >= cap, cand, thr)
    lo = pltpu.bitcast(thr, F32)
    hi = pltpu.bitcast(thr + 1, F32)
    for _ in range(ROUTE_REFINE_STEPS):
        mid = lo + (hi - lo) * 0.5
        keep = count_ge(mid) >= cap
        lo = jnp.where(keep, mid, lo)
        hi = jnp.where(keep, hi, mid)
    gt = a > lo
    eq = a == lo
    need = cap - jnp.sum(jnp.where(gt, 1.0, 0.0), axis=1, keepdims=True)
    ri = lax.broadcasted_iota(jnp.int32, (t, t), 0)
    ci = lax.broadcasted_iota(jnp.int32, (t, t), 1)
    before = jnp.where(ri < ci, 1.0, 0.0).astype(BF16)
    eq_rank = _dot(jnp.where(eq, 1.0, 0.0).astype(BF16), before)
    sel = gt | (eq & (eq_rank < need))
    slot = _dot(jnp.where(sel, 1.0, 0.0).astype(BF16), before)
    slot_ref[...] = jnp.where(sel, slot, -1.0)


def _route(a, cap):
    return pl.pallas_call(
        functools.partial(_route_kernel, cap=cap),
        out_shape=jax.ShapeDtypeStruct(a.shape, F32),
        compiler_params=pltpu.CompilerParams(vmem_limit_bytes=VMEM_LIMIT),
        name="route",
    )(a)


def _gather_kernel(h_ref, slot_ref, aff_ref, xg_ref, g_ref, p_ref, *, cap):
    seqs, _, t = slot_ref.shape
    ci = lax.broadcasted_iota(jnp.int32, (cap, t), 0).astype(F32)
    for sq in range(seqs):
        slots = slice(sq * cap, (sq + 1) * cap)
        for e in range(N_EXPERTS):
            hit = ci == slot_ref[sq, e:e + 1, :]
            p_ref[sq, e * cap:(e + 1) * cap, :] = jnp.where(hit, 1.0, 0.0).astype(BF16)
            g_ref[e, slots, :] = jnp.sum(jnp.where(hit, aff_ref[sq, e:e + 1, :], 0.0), axis=1, keepdims=True)
        xg = _dot(p_ref[sq], h_ref[sq * t:(sq + 1) * t, :])
        for e in range(N_EXPERTS):
            xg_ref[e, slots, :] = xg[e * cap:(e + 1) * cap].astype(BF16)


def _gather(h2, slot, aff, *, t, cap):
    R, D = h2.shape
    nseq = R // t
    seqs = _seqs_per_step(t)
    return pl.pallas_call(
        functools.partial(_gather_kernel, cap=cap),
        grid=(nseq // seqs,),
        in_specs=[
            pl.BlockSpec((seqs * t, D), lambda b: (b, 0)),
            pl.BlockSpec((seqs, N_EXPERTS, t), lambda b: (b, 0, 0)),
            pl.BlockSpec((seqs, N_EXPERTS, t), lambda b: (b, 0, 0)),
        ],
        out_specs=[
            pl.BlockSpec((N_EXPERTS, seqs * cap, D), lambda b: (0, b, 0)),
            pl.BlockSpec((N_EXPERTS, seqs * cap, 1), lambda b: (0, b, 0)),
        ],
        out_shape=[
            jax.ShapeDtypeStruct((N_EXPERTS, nseq * cap, D), BF16),
            jax.ShapeDtypeStruct((N_EXPERTS, nseq * cap, 1), F32),
        ],
        scratch_shapes=[pltpu.VMEM((seqs, N_EXPERTS * cap, t), BF16)],
        compiler_params=_cparams(1),
        name="moe_gather",
    )(h2, slot, aff)


def _expert_kernel(xa_ref, xb_ref, ga_ref, gb_ref, w1_ref, w3_ref, w2_ref, ya_ref, yb_ref, acc_ref):
    f = pl.program_id(1)
    nf = pl.num_programs(1)
    na = xa_ref.shape[0]
    groups = ((xa_ref, ga_ref, ya_ref, 0), (xb_ref, gb_ref, yb_ref, na))

    def body(first, last):
        w1 = w1_ref[...].astype(BF16)
        w3 = w3_ref[...].astype(BF16)
        w2 = w2_ref[...].astype(BF16)
        for x_ref, g_ref, y_ref, base in groups:
            n = x_ref.shape[0]
            sub = min(n, EXPERT_ROW_TILE)
            for r in range(n // sub):
                rows = slice(r * sub, (r + 1) * sub)
                arows = slice(base + r * sub, base + (r + 1) * sub)
                x = x_ref[rows, :]
                a = _dot(x, w1)
                gt = _dot(x, w3)
                hm = (a * _sigmoid(a) * gt).astype(BF16)
                contrib = _dot(hm, w2)
                if not first:
                    contrib = acc_ref[arows, :] + contrib
                if last:
                    y_ref[rows, :] = (contrib * g_ref[rows, :]).astype(BF16)
                else:
                    acc_ref[arows, :] = contrib

    @pl.when(f == 0)
    def _():
        body(True, False)

    @pl.when((f > 0) & (f < nf - 1))
    def _():
        body(False, False)

    @pl.when(f == nf - 1)
    def _():
        body(False, True)


def _experts(xa, xb, ga, gb, w1, w3, w2, l):
    E, na, D = xa.shape
    nb = xb.shape[1]
    FT = EXPERT_F_TILE
    assert D_EXPERT // FT >= 2

    def rows_spec(n, w):
        return pl.BlockSpec((None, n, w), lambda e, f: (e, 0, 0))

    return pl.pallas_call(
        _expert_kernel,
        grid=(E, D_EXPERT // FT),
        in_specs=[
            rows_spec(na, D), rows_spec(nb, D), rows_spec(na, 1), rows_spec(nb, 1),
            pl.BlockSpec((None, None, D, FT), lambda e, f: (l, e, 0, f)),
            pl.BlockSpec((None, None, D, FT), lambda e, f: (l, e, 0, f)),
            pl.BlockSpec((None, None, FT, D), lambda e, f: (l, e, f, 0)),
        ],
        out_specs=[rows_spec(na, D), rows_spec(nb, D)],
        out_shape=[jax.ShapeDtypeStruct((E, na, D), BF16), jax.ShapeDtypeStruct((E, nb, D), BF16)],
        scratch_shapes=[pltpu.VMEM((na + nb, D), F32)],
        compiler_params=_cparams(2),
        name="moe_experts",
    )(xa, xb, ga, gb, w1, w3, w2)


def _scatter_kernel(x_ref, y_ref, slot_ref, mod_ref, *rest, cap):
    o_ref = rest[-1]
    seqs, _, t = slot_ref.shape
    n = N_EXPERTS * cap
    er = lax.broadcasted_iota(jnp.int32, (N_EXPERTS, n), 0)
    ec = lax.broadcasted_iota(jnp.int32, (N_EXPERTS, n), 1)
    expand = jnp.where(ec // cap == er, 1.0, 0.0).astype(BF16)
    want = (lax.broadcasted_iota(jnp.int32, (t, n), 1) % cap).astype(F32)
    for sq in range(seqs):
        rows = slice(sq * t, (sq + 1) * t)
        slot_wide = _dot_tn(slot_ref[sq].astype(BF16), expand)
        pt = jnp.where(slot_wide == want, 1.0, 0.0).astype(BF16)
        y = jnp.concatenate([y_ref[e, sq * cap:(sq + 1) * cap, :] for e in range(N_EXPERTS)], axis=0)
        x2 = x_ref[rows, :] + mod_ref[5] * _dot(pt, y)
        o_ref[rows, :] = x2 if len(rest) == 1 else _rms(x2, rest[0][...])


def _scatter(x1, y, slot, mods, l, *, t, cap, mod_row0, rows_per_mod, final_g=None):
    R, D = x1.shape
    seqs = _seqs_per_step(t)
    assert seqs == 1 or rows_per_mod is None
    in_specs = [
        pl.BlockSpec((seqs * t, D), lambda b: (b, 0)),
        pl.BlockSpec((N_EXPERTS, seqs * cap, D), lambda b: (0, b, 0)),
        pl.BlockSpec((seqs, N_EXPERTS, t), lambda b: (b, 0, 0)),
        _mod_spec(l, mod_row0, rows_per_mod),
    ]
    args = [x1, y, slot, mods]
    if final_g is not None:
        in_specs.append(pl.BlockSpec((1, D), lambda b: (0, 0)))
        args.append(final_g)
    return pl.pallas_call(
        functools.partial(_scatter_kernel, cap=cap),
        grid=(R // (seqs * t),),
        in_specs=in_specs,
        out_specs=pl.BlockSpec((seqs * t, D), lambda b: (b, 0)),
        out_shape=jax.ShapeDtypeStruct((R, D), F32),
        compiler_params=_cparams(1),
        name="moe_scatter",
    )(*args)


def _rope_tables(t_lat):
    half = D_ROPE // 2
    nfreq = half // 2
    pos = np.arange(t_lat)
    inv = 1.0 / (ROPE_BASE ** (np.arange(0, half, 2, dtype=np.float64) / half))
    cos = np.ones((t_lat, HEAD_PAD))
    s_up = np.zeros((t_lat, HEAD_PAD))
    s_dn = np.zeros((t_lat, HEAD_PAD))
    for part, p in enumerate((pos // GRID_W, pos % GRID_W)):
        ang = p[:, None].astype(np.float64) * inv
        o = KR_OFF + part * half
        cos[:, o:o + nfreq] = np.cos(ang)
        cos[:, o + nfreq:o + half] = np.cos(ang)
        s_up[:, o:o + nfreq] = -np.sin(ang)
        s_dn[:, o + nfreq:o + half] = np.sin(ang)
    tab = np.stack([cos, s_up, s_dn], axis=0).reshape(3, t_lat // ROW_TILE, ROW_TILE, HEAD_PAD)
    return jnp.asarray(np.moveaxis(tab, 1, 0), dtype=F32)


def _dft_mats(n, scale):
    k = np.arange(n)
    ang = (2.0 * np.pi / n) * ((k[:, None] * k[None, :]) % n)
    return np.cos(ang) * scale, np.sin(ang) * scale


def _dft_tables(t):
    cw, sw = _dft_mats(FFT_GROUP_W, FFT_GROUP_W ** -0.5)
    eye = np.eye(FFT_GROUPS)
    ct, st = _dft_mats(t, t ** -0.5)

    def const(a):
        return jnp.asarray(a, dtype=F32).astype(BF16)

    return const(np.kron(eye, cw)), const(np.kron(eye, sw)), const(np.concatenate([ct, -st], axis=1))


def _layout_w_in(w_in):
    L, D, _ = w_in.shape
    sizes = (H_MLA * (D_NOPE + D_ROPE), KV_RANK, D_ROPE, H_GLA * GLA_DK, H_GLA * GLA_DK, H_GLA * GLA_DV,
             H_GLA * GLA_DV, 2 * GLA_GATE_RANK, FFT_WIDTH, 3 * D_MODEL)
    offs = np.concatenate([[0], np.cumsum(sizes)])
    seg = [w_in[:, :, offs[i]:offs[i + 1]] for i in range(len(sizes))]
    q, ckv, kr, gq, gk, gv, gr, glr, uf, mg = seg
    q = q.reshape(L, D, H_MLA, D_NOPE + D_ROPE)
    q = jnp.pad(q, ((0, 0), (0, 0), (0, 0), (0, HEAD_PAD - D_NOPE - D_ROPE))).reshape(L, D, H_MLA * HEAD_PAD)
    kr = jnp.pad(kr, ((0, 0), (0, 0), (KR_OFF, HEAD_PAD - KR_OFF - D_ROPE)))
    glr = jnp.pad(glr, ((0, 0), (0, 0), (0, LANE - 2 * GLA_GATE_RANK)))
    cols = jnp.concatenate([q, ckv, kr, glr, uf, gq, gk, gv, gr, mg], axis=-1).astype(BF16)
    return jnp.moveaxis(cols.reshape(L, D, N_CHUNK, D), 2, 1)


def _layout_w_ukv(w_ukv):
    L = w_ukv.shape[0]
    w = w_ukv.reshape(L, KV_RANK, H_MLA, D_NOPE + D_V)
    k = jnp.pad(w[..., :D_NOPE], ((0, 0), (0, 0), (0, 0), (0, HEAD_PAD - D_NOPE)))
    v = w[..., D_NOPE:].reshape(L, KV_RANK, H_MLA // 2, 2, D_V)
    z = jnp.zeros_like(v[:, :, :, 0])
    v_even = jnp.concatenate([v[:, :, :, 0], z], axis=-1)
    v_odd = jnp.concatenate([z, v[:, :, :, 1]], axis=-1)
    v2 = jnp.stack([v_even, v_odd], axis=3)
    return (k.reshape(L, KV_RANK, H_MLA * HEAD_PAD).astype(BF16),
            v2.reshape(L, KV_RANK, H_MLA * HEAD_PAD).astype(BF16))


def _layout_gate(gate_w, gate_b):
    L = gate_w.shape[0]
    HW = H_GLA * GLA_DK
    wg = jnp.zeros((L, LANE, 2 * HW), F32)
    for d in range(2):
        wg = wg.at[:, d * GLA_GATE_RANK:(d + 1) * GLA_GATE_RANK, d * HW:(d + 1) * HW].set(gate_w[:, d])
    return wg.astype(BF16), gate_b.reshape(L, 1, 2 * HW)


def kernel(x_prompt, x_sample, cache_ckv, cache_krope, state_gla, c, c_ctx, ada_w, ada_b, norm1_g, norm2_g, w_in, mla_kv_norm_g, mla_w_ukv, gla_gate_w, gla_gate_b, gla_norm_g, w_mla_out, w_gla_out, w_fft_out, w_o, router_w, exp_w1, exp_w3, exp_w2, final_norm_g):
    nb_c, t_c, D = x_prompt.shape
    nb_l, t_l, _ = x_sample.shape
    assert (nb_c * t_c) % ROW_TILE == 0 and ROW_TILE % t_c == 0 and t_l % ROW_TILE == 0 and 1 + nb_l <= MOD_ROWS
    cap_c = CAPACITY_FACTOR * t_c // N_EXPERTS
    cap_l = CAPACITY_FACTOR * t_l // N_EXPERTS

    c_all = jnp.zeros((MOD_ROWS, D), F32).at[0].set(c_ctx).at[1:1 + nb_l].set(c)
    mods = _mods(c_all, ada_w, ada_b).reshape(DEPTH, N_MOD, MOD_ROWS, 1, D)
    rope_tab = _rope_tables(t_l)
    dft_c = _dft_tables(t_c)
    dft_l = _dft_tables(t_l)
    w_in_r = _layout_w_in(w_in)
    wuk, wuv = _layout_w_ukv(mla_w_ukv)
    wgate, bgate = _layout_gate(gla_gate_w, gla_gate_b)
    n1 = norm1_g.reshape(DEPTH, 1, D)
    n2 = norm2_g.reshape(DEPTH, 1, D)
    kvg = mla_kv_norm_g.reshape(DEPTH, 1, KV_RANK)
    glg = gla_norm_g.reshape(DEPTH, 1, GLA_DV)
    wm = w_mla_out.astype(BF16)
    wgo = w_gla_out.astype(BF16)
    wfo = w_fft_out.astype(BF16)
    wo = w_o.astype(BF16)
    rw_hi = router_w.astype(BF16)
    rw_lo = (router_w - rw_hi.astype(F32)).astype(BF16)
    rwt = jnp.zeros((DEPTH, D, 2 * LANE), BF16).at[:, :, :N_EXPERTS].set(rw_hi)
    rwt = rwt.at[:, :, LANE:LANE + N_EXPERTS].set(rw_lo)
    ckr_pad = jnp.pad(cache_krope, ((0, 0), (0, 0), (0, 0), (KR_OFF, HEAD_PAD - KR_OFF - D_ROPE)))

    ctx = dict(t=t_c, cap=cap_c, mod_row0=0, rows_per_mod=None)
    lat = dict(t=t_l, cap=cap_l, mod_row0=1, rows_per_mod=t_l // ROW_TILE)
    xs = [x_prompt.reshape(nb_c * t_c, D), x_sample.reshape(nb_l * t_l, D)]
    ckv_out, kr_out, st_out = [], [], []

    def pre_experts(x, grp, l, is_ctx):
        t, cap = grp["t"], grp["cap"]
        mod_kw = dict(mod_row0=grp["mod_row0"], rows_per_mod=grp["rows_per_mod"])
        proj, ckv, krp = _inproj(x, mods, n1, kvg, w_in_r, l, rope_tab=None if is_ctx else rope_tab, **mod_kw)
        if is_ctx:
            ckv_out.append(ckv.reshape(nb_c, t_c, KV_RANK))
            kr_out.append(krp[:, KR_OFF:KR_OFF + D_ROPE].reshape(nb_c, t_c, D_ROPE))
            o_mla = _attn(proj, ckv, krp, wuk, wuv, l, t_q=t)
            o_gla, st = _gla(proj, wgate, bgate, glg, l, t=t, want_state=True)
            st_out.append(st)
            o_fft = _fft(proj, *dft_c, t=t)
        else:
            o_mla = _attn(proj, ckv, krp, wuk, wuv, l, t_q=t, cache=(cache_ckv, ckr_pad))
            (o_gla,) = _gla(proj, wgate, bgate, glg, l, t=t, state=state_gla)
            o_fft = _fft(proj, *dft_l, t=t)
        rows_per_mod_mix = grp["rows_per_mod"]
        x1, h2, aff = _mix(x, o_mla, o_gla, o_fft, proj, mods, n2, wm, wgo, wfo, wo, rwt, l, t=t,
                           mod_row0=grp["mod_row0"], rows_per_mod=rows_per_mod_mix)
        nseq = aff.shape[0]
        slot = _route(aff.reshape(nseq * N_EXPERTS, t), cap).reshape(nseq, N_EXPERTS, t)
        xg, g = _gather(h2, slot, aff, t=t, cap=cap)
        return x1, slot, xg, g

    for l in range(DEPTH):
        x1_c, slot_c, xg_c, g_c = pre_experts(xs[0], ctx, l, True)
        x1_l, slot_l, xg_l, g_l = pre_experts(xs[1], lat, l, False)
        y_c, y_l = _experts(xg_c, xg_l, g_c, g_l, exp_w1, exp_w3, exp_w2, l)
        fg = final_norm_g.reshape(1, D) if l == DEPTH - 1 else None
        xs = [
            _scatter(x1_c, y_c, slot_c, mods, l, t=t_c, cap=cap_c, mod_row0=0, rows_per_mod=None, final_g=fg),
            _scatter(x1_l, y_l, slot_l, mods, l, t=t_l, cap=cap_l, mod_row0=1, rows_per_mod=1, final_g=fg),
        ]

    y_prompt = xs[0].reshape(nb_c, t_c, D)
    y_sample = xs[1].reshape(nb_l, t_l, D)
    return (y_prompt, y_sample, jnp.stack(ckv_out, axis=1), jnp.stack(kr_out, axis=1),
            jnp.stack(st_out, axis=1))
```

```python
import functools
import math

import jax
import jax.numpy as jnp
import numpy as np
from jax import lax
from jax.experimental import pallas as pl
from jax.experimental.pallas import tpu as pltpu

F32 = jnp.float32
BF16 = jnp.bfloat16

D_MODEL = 1024
DEPTH = 4
GRID_W = 64
H_MLA = 8
D_NOPE = 64
D_ROPE = 32
D_V = 64
KV_RANK = 256
ROPE_BASE = 10000.0
MLA_SCALE = (D_NOPE + D_ROPE) ** -0.5
H_GLA = 4
GLA_DK = 128
GLA_DV = 128
GLA_GATE_RANK = 16
GLA_TAU = 16.0
GLA_CHUNK = 64
FFT_GROUPS = 4
FFT_GROUP_W = 128
FFT_WIDTH = FFT_GROUPS * FFT_GROUP_W
N_EXPERTS = 16
D_EXPERT = 2048
CAPACITY_FACTOR = 2
N_MOD = 6
EPS = 1e-6

LANE = 128
ROW_TILE = 512
HEAD_PAD = 128
KR_OFF = D_NOPE
N_CHUNK = 7
MOD_ROWS = 16
MOD_PER_STEP = 2
EXPERT_F_TILE = 512
EXPERT_ROW_TILE = 512
ATTN_Q_TILE = 512
STACK_SEQS = 2
SHORT_SEQ_ROWS = 1024
ROUTE_REFINE_STEPS = 16
Q_PRESCALE = MLA_SCALE * math.log2(math.e)
VMEM_LIMIT = 56 * 1024 * 1024


def _cparams(n_axes):
    return pltpu.CompilerParams(dimension_semantics=("arbitrary",) * n_axes, vmem_limit_bytes=VMEM_LIMIT)


def _dot(a, b):
    return jnp.dot(a, b, preferred_element_type=F32)


def _dot_nt(a, b):
    return lax.dot_general(a, b, (((1,), (1,)), ((), ())), preferred_element_type=F32)


def _dot_tn(a, b):
    return lax.dot_general(a, b, (((0,), (0,)), ((), ())), preferred_element_type=F32)


def _sigmoid(x):
    return 1.0 / (1.0 + jnp.exp(-x))


def _rms(x, g):
    return x * lax.rsqrt(jnp.mean(x * x, axis=-1, keepdims=True) + EPS) * g


def _mods_kernel(c_ref, w_ref, b_ref, o_ref):
    c = c_ref[...]
    s = (c * _sigmoid(c)).astype(BF16)
    m = _dot(s, w_ref[...].astype(BF16))
    for k in range(MOD_PER_STEP):
        o_ref[k] = m[:, k * D_MODEL:(k + 1) * D_MODEL] + b_ref[k]


def _mods(c_all, ada_w, ada_b):
    D = D_MODEL
    return pl.pallas_call(
        _mods_kernel,
        grid=(DEPTH, N_MOD // MOD_PER_STEP),
        in_specs=[
            pl.BlockSpec((MOD_ROWS, D), lambda l, k: (0, 0)),
            pl.BlockSpec((None, D, MOD_PER_STEP * D), lambda l, k: (l, 0, k)),
            pl.BlockSpec((None, MOD_PER_STEP, 1, D), lambda l, k: (l, k, 0, 0)),
        ],
        out_specs=pl.BlockSpec((None, MOD_PER_STEP, MOD_ROWS, D), lambda l, k: (l, k, 0, 0)),
        out_shape=jax.ShapeDtypeStruct((DEPTH, N_MOD, MOD_ROWS, D), F32),
        compiler_params=_cparams(2),
        name="adaln_mods",
    )(c_all, ada_w, ada_b.reshape(DEPTH, N_MOD, 1, D))


def _mod_spec(l, mod_row0, rows_per_mod):
    if rows_per_mod is None:
        return pl.BlockSpec((None, N_MOD, None, 1, D_MODEL), lambda i: (l, 0, mod_row0, 0, 0))
    return pl.BlockSpec((None, N_MOD, None, 1, D_MODEL), lambda i: (l, 0, mod_row0 + i // rows_per_mod, 0, 0))


def _inproj_kernel(*refs, has_rope):
    if has_rope:
        x_ref, mod_ref, n1_ref, kvg_ref, w_ref, rope_ref, proj_ref, ckv_ref, krp_ref = refs
        cos, sin_up, sin_dn = rope_ref[0], rope_ref[1], rope_ref[2]

        def rope(a):
            return a * cos + pltpu.roll(a, LANE - 8, 1) * sin_up + pltpu.roll(a, 8, 1) * sin_dn
    else:
        x_ref, mod_ref, n1_ref, kvg_ref, w_ref, proj_ref, ckv_ref, krp_ref = refs

        def rope(a):
            return a

    x = x_ref[...]
    h = _rms(x, n1_ref[...]) * (1.0 + mod_ref[1]) + mod_ref[0]
    hb = h.astype(BF16)

    acc = _dot(hb, w_ref[0])
    for hd in range(H_MLA):
        sl = slice(hd * HEAD_PAD, (hd + 1) * HEAD_PAD)
        proj_ref[0, :, sl] = (rope(acc[:, sl]) * Q_PRESCALE).astype(BF16)

    acc = _dot(hb, w_ref[1])
    ckv_ref[...] = _rms(acc[:, :KV_RANK], kvg_ref[...])
    krp_ref[...] = rope(acc[:, KV_RANK:KV_RANK + HEAD_PAD])
    proj_ref[1] = acc.astype(BF16)

    for c in (2, 3):
        proj_ref[c] = _dot(hb, w_ref[c]).astype(BF16)
    for c in (4, 5, 6):
        proj_ref[c] = _sigmoid(_dot(hb, w_ref[c])).astype(BF16)


def _inproj(x, mods, n1, kvg, w_in_r, l, *, mod_row0, rows_per_mod, rope_tab=None):
    R, D = x.shape
    in_specs = [
        pl.BlockSpec((ROW_TILE, D), lambda i: (i, 0)),
        _mod_spec(l, mod_row0, rows_per_mod),
        pl.BlockSpec((None, 1, D), lambda i: (l, 0, 0)),
        pl.BlockSpec((None, 1, KV_RANK), lambda i: (l, 0, 0)),
        pl.BlockSpec((None, N_CHUNK, D, D), lambda i: (l, 0, 0, 0), pipeline_mode=pl.Buffered(1)),
    ]
    args = [x, mods, n1, kvg, w_in_r]
    if rope_tab is not None:
        nper = rope_tab.shape[0]
        in_specs.append(pl.BlockSpec((None, 3, ROW_TILE, HEAD_PAD), lambda i: (i % nper, 0, 0, 0)))
        args.append(rope_tab)
    return pl.pallas_call(
        functools.partial(_inproj_kernel, has_rope=rope_tab is not None),
        grid=(R // ROW_TILE,),
        in_specs=in_specs,
        out_specs=[
            pl.BlockSpec((N_CHUNK, ROW_TILE, D), lambda i: (0, i, 0)),
            pl.BlockSpec((ROW_TILE, KV_RANK), lambda i: (i, 0)),
            pl.BlockSpec((ROW_TILE, HEAD_PAD), lambda i: (i, 0)),
        ],
        out_shape=[
            jax.ShapeDtypeStruct((N_CHUNK, R, D), BF16),
            jax.ShapeDtypeStruct((R, KV_RANK), F32),
            jax.ShapeDtypeStruct((R, HEAD_PAD), F32),
        ],
        compiler_params=_cparams(1),
        name="inproj",
    )(*args)


def _attn_kernel(*refs, t_q, seqs, has_cache, q_blk):
    if has_cache:
        q_ref, ckv_ref, krp_ref, cckv_ref, ckr_ref, wuk_ref, wuv_ref, o_ref = refs
        ckv = jnp.concatenate([cckv_ref[...], ckv_ref[...]], axis=0)
        krp = jnp.concatenate([ckr_ref[...], krp_ref[...]], axis=0)
    else:
        q_ref, ckv_ref, krp_ref, wuk_ref, wuv_ref, o_ref = refs
        ckv = ckv_ref[...]
        krp = krp_ref[...]
    t_k = ckv.shape[0] // seqs
    ckvb = ckv.astype(BF16)
    k_all = _dot(ckvb, wuk_ref[...])
    v_all = _dot(ckvb, wuv_ref[...]).astype(BF16)
    for sq in range(seqs):
        keys = slice(sq * t_k, (sq + 1) * t_k)
        for pair in range(H_MLA // 2):
            k_pair = [(k_all[keys, hd * HEAD_PAD:(hd + 1) * HEAD_PAD] + krp[keys]).astype(BF16)
                      for hd in (2 * pair, 2 * pair + 1)]
            for qb in range(t_q // q_blk):
                rows = slice(sq * t_q + qb * q_blk, sq * t_q + (qb + 1) * q_blk)
                o_pair = None
                for j, hd in enumerate((2 * pair, 2 * pair + 1)):
                    sl = slice(hd * HEAD_PAD, (hd + 1) * HEAD_PAD)
                    s = _dot_nt(q_ref[rows, sl], k_pair[j])
                    p = jnp.exp2(s - jnp.max(s, axis=-1, keepdims=True))
                    den = jnp.sum(p, axis=-1, keepdims=True)
                    o_h = _dot(p.astype(BF16), v_all[keys, sl]) / den
                    o_pair = o_h if o_pair is None else o_pair + o_h
                o_ref[rows, pair * LANE:(pair + 1) * LANE] = o_pair.astype(BF16)


def _seqs_per_step(t, has_cache=False):
    return 1 if has_cache else max(1, SHORT_SEQ_ROWS // t)


def _attn(proj, ckv, krp, wuk, wuv, l, *, t_q, cache=None):
    R = ckv.shape[0]
    seqs = _seqs_per_step(t_q, cache is not None)
    t_q_seq, t_q = t_q, t_q * seqs
    in_specs = [
        pl.BlockSpec((None, t_q, D_MODEL), lambda b: (0, b, 0)),
        pl.BlockSpec((t_q, KV_RANK), lambda b: (b, 0)),
        pl.BlockSpec((t_q, HEAD_PAD), lambda b: (b, 0)),
    ]
    args = [proj, ckv, krp]
    if cache is not None:
        cckv, ckr = cache
        past = cckv.shape[2]
        in_specs += [
            pl.BlockSpec((None, None, past, KV_RANK), lambda b: (b, l, 0, 0)),
            pl.BlockSpec((None, None, past, HEAD_PAD), lambda b: (b, l, 0, 0)),
        ]
        args += [cckv, ckr]
    in_specs += [
        pl.BlockSpec((None, KV_RANK, H_MLA * HEAD_PAD), lambda b: (l, 0, 0)),
        pl.BlockSpec((None, KV_RANK, H_MLA * HEAD_PAD), lambda b: (l, 0, 0)),
    ]
    args += [wuk, wuv]
    return pl.pallas_call(
        functools.partial(_attn_kernel, t_q=t_q_seq, seqs=seqs, has_cache=cache is not None,
                          q_blk=min(t_q_seq, ATTN_Q_TILE)),
        grid=(R // t_q,),
        in_specs=in_specs,
        out_specs=pl.BlockSpec((t_q, H_MLA * D_V), lambda b: (b, 0)),
        out_shape=jax.ShapeDtypeStruct((R, H_MLA * D_V), BF16),
        compiler_params=_cparams(1),
        name="mla_attn",
    )(*args)


def _gla_kernel(*refs, t, seqs, has_state, want_state):
    refs = list(refs)
    q_ref, k_ref, v_ref, r_ref, glr_ref, wg_ref, bg_ref, g_ref = refs[:8]
    pos = 8
    s0_ref = None
    if has_state:
        s0_ref = refs[pos]
        pos += 1
    o_ref = refs[pos]
    pos += 1
    so_ref = None
    if want_state:
        so_ref = refs[pos]
        pos += 1
    b_ref, of_ref, ob_ref, st_ref = refs[pos:pos + 4]

    CH = GLA_CHUNK
    BLK = 256
    HW = H_GLA * GLA_DK

    pre = _dot(glr_ref[...], wg_ref[...]) + bg_ref[...]
    la = (jnp.minimum(pre, 0.0) - jnp.log(1.0 + jnp.exp(-jnp.abs(pre)))) * (1.0 / GLA_TAU)
    la_hi = la.astype(BF16)
    la_lo = (la - la_hi.astype(F32)).astype(BF16)
    ri = lax.broadcasted_iota(jnp.int32, (BLK, BLK), 0)
    ci = lax.broadcasted_iota(jnp.int32, (BLK, BLK), 1)
    same = (ri // CH) == (ci // CH)
    tri_f = jnp.where(same & (ci <= ri), 1.0, 0.0).astype(BF16)
    tri_b = jnp.where(same & (ci >= ri), 1.0, 0.0).astype(BF16)
    for blk in range(t // BLK):
        rows = slice(blk * BLK, (blk + 1) * BLK)
        hi, lo = la_hi[rows], la_lo[rows]
        b_ref[rows, :HW] = _dot(tri_f, hi[:, :HW]) + _dot(tri_f, lo[:, :HW])
        b_ref[rows, HW:] = _dot(tri_b, hi[:, HW:]) + _dot(tri_b, lo[:, HW:])

    n_st = 2 * H_GLA
    for sq in range(seqs):
        for h in range(H_GLA):
            for d in range(2):
                if has_state:
                    st_ref[sq * n_st + 2 * h + d] = s0_ref[d, h].T
                else:
                    st_ref[sq * n_st + 2 * h + d] = jnp.zeros((GLA_DV, GLA_DK), F32)

    masks = (same & (ci <= ri), same & (ci >= ri))
    scale = GLA_DK ** -0.5
    CPB = BLK // CH
    zero_chunk = jnp.zeros((CH, GLA_DK), BF16)

    def chunk_diag(x):
        return jnp.concatenate(
            [jnp.concatenate([x[c * CH:(c + 1) * CH] if j == c else zero_chunk for j in range(CPB)], axis=1)
             for c in range(CPB)], axis=0)

    def unit(sq, h, d, blk, vt):
        rows = slice(blk * BLK, (blk + 1) * BLK)
        cols = slice(h * GLA_DK, (h + 1) * GLA_DK)
        b = b_ref[rows, d * HW + h * GLA_DK:d * HW + (h + 1) * GLA_DK]
        last = [b[c * CH + CH - 1:c * CH + CH] if d == 0 else b[c * CH:c * CH + 1] for c in range(CPB)]
        b_last = jnp.concatenate([jnp.broadcast_to(x, (CH, GLA_DK)) for x in last], axis=0)
        q = q_ref[rows, cols].astype(F32) * scale
        k = k_ref[rows, cols].astype(F32)
        q_t = (q * jnp.exp(b)).astype(BF16)
        k_t = (k * jnp.exp(-b)).astype(BF16)
        k_end = (k * jnp.exp(b_last - b)).astype(BF16)
        att = jnp.where(masks[d], _dot_nt(q_t, k_t), 0.0).astype(BF16)
        ds = _dot(vt, chunk_diag(k_end))
        st = st_ref[sq * n_st + 2 * h + d]
        prev = [None] * CPB
        for c in (range(CPB) if d == 0 else range(CPB - 1, -1, -1)):
            prev[c] = st.astype(BF16)
            st = st * jnp.exp(last[c]) + ds[:, c * GLA_DK:(c + 1) * GLA_DK]
        st_ref[sq * n_st + 2 * h + d] = st
        o = _dot_nt(jnp.concatenate([att, chunk_diag(q_t)], axis=1),
                    jnp.concatenate([vt] + prev, axis=1))
        if d == 0:
            of_ref[rows, cols] = o
        else:
            ob_ref[rows, cols] = o

    n_blk = t // seqs // BLK
    for n in range(n_blk):
        for sq in range(seqs):
            for h in range(H_GLA):
                cols = slice(h * GLA_DV, (h + 1) * GLA_DV)
                for d, blk in ((0, sq * n_blk + n), (1, sq * n_blk + n_blk - 1 - n)):
                    vt = v_ref[blk * BLK:(blk + 1) * BLK, cols].astype(F32).T.astype(BF16)
                    unit(sq, h, d, blk, vt)

    if want_state:
        for sq in range(seqs):
            for h in range(H_GLA):
                for d in range(2):
                    so_ref[sq, d, h] = st_ref[sq * n_st + 2 * h + d].T

    g = g_ref[...]
    for h in range(H_GLA):
        cols = slice(h * GLA_DV, (h + 1) * GLA_DV)
        o = _rms(of_ref[:, cols] + ob_ref[:, cols], g)
        r = r_ref[:, cols].astype(F32)
        o_ref[:, cols] = (o * (r * _sigmoid(r))).astype(BF16)


def _gla(proj, glr_w, glr_b, gla_g, l, *, t, state=None, want_state=False):
    R = proj.shape[1]
    HW = H_GLA * GLA_DK
    nseq = R // t
    seqs = _seqs_per_step(t, state is not None)
    t = t * seqs

    def spec(chunk, half):
        return pl.BlockSpec((None, t, HW), lambda b: (chunk, b, half))

    in_specs = [
        spec(2, 0), spec(2, 1), spec(3, 0), spec(3, 1),
        pl.BlockSpec((None, t, LANE), lambda b: (1, b, 3)),
        pl.BlockSpec((None, LANE, 2 * HW), lambda b: (l, 0, 0)),
        pl.BlockSpec((None, 1, 2 * HW), lambda b: (l, 0, 0)),
        pl.BlockSpec((None, 1, GLA_DV), lambda b: (l, 0, 0)),
    ]
    args = [proj, proj, proj, proj, proj, glr_w, glr_b, gla_g]
    if state is not None:
        in_specs.append(pl.BlockSpec((None, None, 2, H_GLA, GLA_DK, GLA_DV), lambda b: (b, l, 0, 0, 0, 0)))
        args.append(state)
    out_specs = [pl.BlockSpec((t, HW), lambda b: (b, 0))]
    out_shape = [jax.ShapeDtypeStruct((R, HW), BF16)]
    if want_state:
        out_specs.append(pl.BlockSpec((seqs, 2, H_GLA, GLA_DK, GLA_DV), lambda b: (b, 0, 0, 0, 0)))
        out_shape.append(jax.ShapeDtypeStruct((nseq, 2, H_GLA, GLA_DK, GLA_DV), F32))
    return pl.pallas_call(
        functools.partial(_gla_kernel, t=t, seqs=seqs, has_state=state is not None, want_state=want_state),
        grid=(nseq // seqs,),
        in_specs=in_specs,
        out_specs=out_specs,
        out_shape=out_shape,
        scratch_shapes=[pltpu.VMEM((t, 2 * HW), F32), pltpu.VMEM((t, HW), F32), pltpu.VMEM((t, HW), F32),
                        pltpu.VMEM((seqs * 2 * H_GLA, GLA_DV, GLA_DK), F32)],
        compiler_params=_cparams(1),
        name="gla",
    )(*args)


def _fft_kernel(u_ref, cw_ref, sw_ref, ct_ref, o_ref, *, t):
    u = u_ref[...]
    a = _dot(u, cw_ref[...]).astype(BF16)
    b = _dot(u, sw_ref[...]).astype(BF16)
    for sq in range(u.shape[0] // t):
        rows = slice(sq * t, (sq + 1) * t)
        ab = jnp.concatenate([a[rows], b[rows]], axis=0)
        o_ref[rows, :] = _dot(ct_ref[...], ab).astype(BF16)


def _fft(proj, cw, sw, ct, *, t):
    R = proj.shape[1]
    W = FFT_WIDTH
    rows = t * _seqs_per_step(t)
    return pl.pallas_call(
        functools.partial(_fft_kernel, t=t),
        grid=(R // rows,),
        in_specs=[
            pl.BlockSpec((None, rows, W), lambda b: (1, b, 1)),
            pl.BlockSpec((W, W), lambda b: (0, 0)),
            pl.BlockSpec((W, W), lambda b: (0, 0)),
            pl.BlockSpec((t, 2 * t), lambda b: (0, 0)),
        ],
        out_specs=pl.BlockSpec((rows, W), lambda b: (b, 0)),
        out_shape=jax.ShapeDtypeStruct((R, W), BF16),
        compiler_params=_cparams(1),
        name="fourier_mix",
    )(proj, cw, sw, ct)


def _mix_kernel(x_ref, om_ref, og_ref, of_ref, g0_ref, g1_ref, g2_ref, mod_ref, n2_ref,
                wm_ref, wg_ref, wf_ref, wo_ref, rw_ref, x1_ref, h2_ref, aff_ref, *, seqs_per_tile):
    ym = _dot(om_ref[...], wm_ref[...])
    yg = _dot(og_ref[...], wg_ref[...])
    yf = _dot(of_ref[...], wf_ref[...])
    merged = g0_ref[...].astype(F32) * ym + g1_ref[...].astype(F32) * yg + g2_ref[...].astype(F32) * yf
    z = _dot(merged.astype(BF16), wo_ref[...])
    x1 = x_ref[...] + mod_ref[2] * z
    x1_ref[...] = x1
    h2 = _rms(x1, n2_ref[...]) * (1.0 + mod_ref[4]) + mod_ref[3]
    h_hi = h2.astype(BF16)
    h2_ref[...] = h_hi
    h_lo = (h2 - h_hi.astype(F32)).astype(BF16)
    lg = _dot(h_hi, rw_ref[...])
    lg = lg[:, :LANE] + lg[:, LANE:] + _dot(h_lo, rw_ref[:, :LANE])
    logits = lg.T[:N_EXPERTS]
    m = jnp.max(logits, axis=0, keepdims=True)
    p = jnp.exp(logits - m)
    aff = p / jnp.sum(p, axis=0, keepdims=True)
    if seqs_per_tile == 0:
        aff_ref[...] = aff
    else:
        t = ROW_TILE // seqs_per_tile
        for s in range(seqs_per_tile):
            aff_ref[s] = aff[:, s * t:(s + 1) * t]


def _mix(x, o_mla, o_gla, o_fft, proj, mods, n2, wm, wg, wf, wo, rwt, l, *, t, mod_row0, rows_per_mod):
    R, D = x.shape
    HW = H_MLA * D_V
    nseq = R // t

    def wspec(k):
        return pl.BlockSpec((None, k, D), lambda i: (l, 0, 0))

    if t <= ROW_TILE:
        seqs_per_tile = ROW_TILE // t
        aff_spec = pl.BlockSpec((seqs_per_tile, N_EXPERTS, t), lambda i: (i, 0, 0))
    else:
        seqs_per_tile = 0
        tiles = t // ROW_TILE
        aff_spec = pl.BlockSpec((None, N_EXPERTS, ROW_TILE), lambda i: (i // tiles, 0, i % tiles))
    return pl.pallas_call(
        functools.partial(_mix_kernel, seqs_per_tile=seqs_per_tile),
        grid=(R // ROW_TILE,),
        in_specs=[
            pl.BlockSpec((ROW_TILE, D), lambda i: (i, 0)),
            pl.BlockSpec((ROW_TILE, HW), lambda i: (i, 0)),
            pl.BlockSpec((ROW_TILE, HW), lambda i: (i, 0)),
            pl.BlockSpec((ROW_TILE, HW), lambda i: (i, 0)),
            pl.BlockSpec((None, ROW_TILE, D), lambda i: (4, i, 0)),
            pl.BlockSpec((None, ROW_TILE, D), lambda i: (5, i, 0)),
            pl.BlockSpec((None, ROW_TILE, D), lambda i: (6, i, 0)),
            _mod_spec(l, mod_row0, rows_per_mod),
            pl.BlockSpec((None, 1, D), lambda i: (l, 0, 0)),
            wspec(HW), wspec(HW), wspec(HW), wspec(D),
            pl.BlockSpec((None, D, 2 * LANE), lambda i: (l, 0, 0)),
        ],
        out_specs=[
            pl.BlockSpec((ROW_TILE, D), lambda i: (i, 0)),
            pl.BlockSpec((ROW_TILE, D), lambda i: (i, 0)),
            aff_spec,
        ],
        out_shape=[
            jax.ShapeDtypeStruct((R, D), F32),
            jax.ShapeDtypeStruct((R, D), BF16),
            jax.ShapeDtypeStruct((nseq, N_EXPERTS, t), F32),
        ],
        compiler_params=_cparams(1),
        name="mix_out",
    )(x, o_mla, o_gla, o_fft, proj, proj, proj, mods, n2, wm, wg, wf, wo, rwt)


def _route_kernel(a_ref, slot_ref, *, cap):
    a = a_ref[...]
    rows, t = a.shape

    def count_ge(x):
        return jnp.sum(jnp.where(a >= x, 1.0, 0.0), axis=1, keepdims=True)

    thr = jnp.zeros((rows, 1), jnp.int32)
    for bit in range(30, -1, -1):
        cand = thr | (1 << bit)
        thr = jnp.where(count_ge(pltpu.bitcast(cand, F32)) >= cap, cand, thr)
    lo = pltpu.bitcast(thr, F32)
    hi = pltpu.bitcast(thr + 1, F32)
    for _ in range(ROUTE_REFINE_STEPS):
        mid = lo + (hi - lo) * 0.5
        keep = count_ge(mid) >= cap
        lo = jnp.where(keep, mid, lo)
        hi = jnp.where(keep, hi, mid)
    gt = a > lo
    eq = a == lo
    need = cap - jnp.sum(jnp.where(gt, 1.0, 0.0), axis=1, keepdims=True)
    ri = lax.broadcasted_iota(jnp.int32, (t, t), 0)
    ci = lax.broadcasted_iota(jnp.int32, (t, t), 1)
    before = jnp.where(ri < ci, 1.0, 0.0).astype(BF16)
    eq_rank = _dot(jnp.where(eq, 1.0, 0.0).astype(BF16), before)
    sel = gt | (eq & (eq_rank < need))
    slot = _dot(jnp.where(sel, 1.0, 0.0).astype(BF16), before)
    slot_ref[...] = jnp.where(sel, slot, -1.0)


def _route(a, cap):
    return pl.pallas_call(
        functools.partial(_route_kernel, cap=cap),
        out_shape=jax.ShapeDtypeStruct(a.shape, F32),
        compiler_params=pltpu.CompilerParams(vmem_limit_bytes=VMEM_LIMIT),
        name="route",
    )(a)


def _gather_kernel(h_ref, slot_ref, aff_ref, xg_ref, g_ref, p_ref, *, cap):
    seqs, _, t = slot_ref.shape
    ci = lax.broadcasted_iota(jnp.int32, (cap, t), 0).astype(F32)
    for sq in range(seqs):
        slots = slice(sq * cap, (sq + 1) * cap)
        for e in range(N_EXPERTS):
            hit = ci == slot_ref[sq, e:e + 1, :]
            p_ref[sq, e * cap:(e + 1) * cap, :] = jnp.where(hit, 1.0, 0.0).astype(BF16)
            g_ref[e, slots, :] = jnp.sum(jnp.where(hit, aff_ref[sq, e:e + 1, :], 0.0), axis=1, keepdims=True)
        xg = _dot(p_ref[sq], h_ref[sq * t:(sq + 1) * t, :])
        for e in range(N_EXPERTS):
            xg_ref[e, slots, :] = xg[e * cap:(e + 1) * cap].astype(BF16)


def _gather(h2, slot, aff, *, t, cap):
    R, D = h2.shape
    nseq = R // t
    seqs = _seqs_per_step(t)
    return pl.pallas_call(
        functools.partial(_gather_kernel, cap=cap),
        grid=(nseq // seqs,),
        in_specs=[
            pl.BlockSpec((seqs * t, D), lambda b: (b, 0)),
            pl.BlockSpec((seqs, N_EXPERTS, t), lambda b: (b, 0, 0)),
            pl.BlockSpec((seqs, N_EXPERTS, t), lambda b: (b, 0, 0)),
        ],
        out_specs=[
            pl.BlockSpec((N_EXPERTS, seqs * cap, D), lambda b: (0, b, 0)),
            pl.BlockSpec((N_EXPERTS, seqs * cap, 1), lambda b: (0, b, 0)),
        ],
        out_shape=[
            jax.ShapeDtypeStruct((N_EXPERTS, nseq * cap, D), BF16),
            jax.ShapeDtypeStruct((N_EXPERTS, nseq * cap, 1), F32),
        ],
        scratch_shapes=[pltpu.VMEM((seqs, N_EXPERTS * cap, t), BF16)],
        compiler_params=_cparams(1),
        name="moe_gather",
    )(h2, slot, aff)


def _expert_kernel(xa_ref, xb_ref, ga_ref, gb_ref, w1_ref, w3_ref, w2_ref, ya_ref, yb_ref, acc_ref):
    f = pl.program_id(1)
    nf = pl.num_programs(1)
    na = xa_ref.shape[0]
    groups = ((xa_ref, ga_ref, ya_ref, 0), (xb_ref, gb_ref, yb_ref, na))

    def body(first, last):
        w1 = w1_ref[...].astype(BF16)
        w3 = w3_ref[...].astype(BF16)
        w2 = w2_ref[...].astype(BF16)
        for x_ref, g_ref, y_ref, base in groups:
            n = x_ref.shape[0]
            sub = min(n, EXPERT_ROW_TILE)
            for r in range(n // sub):
                rows = slice(r * sub, (r + 1) * sub)
                arows = slice(base + r * sub, base + (r + 1) * sub)
                x = x_ref[rows, :]
                a = _dot(x, w1)
                gt = _dot(x, w3)
                hm = (a * _sigmoid(a) * gt).astype(BF16)
                contrib = _dot(hm, w2)
                if not first:
                    contrib = acc_ref[arows, :] + contrib
                if last:
                    y_ref[rows, :] = (contrib * g_ref[rows, :]).astype(BF16)
                else:
                    acc_ref[arows, :] = contrib

    @pl.when(f == 0)
    def _():
        body(True, False)

    @pl.when((f > 0) & (f < nf - 1))
    def _():
        body(False, False)

    @pl.when(f == nf - 1)
    def _():
        body(False, True)


def _experts(xa, xb, ga, gb, w1, w3, w2, l):
    E, na, D = xa.shape
    nb = xb.shape[1]
    FT = EXPERT_F_TILE
    assert D_EXPERT // FT >= 2

    def rows_spec(n, w):
        return pl.BlockSpec((None, n, w), lambda e, f: (e, 0, 0))

    return pl.pallas_call(
        _expert_kernel,
        grid=(E, D_EXPERT // FT),
        in_specs=[
            rows_spec(na, D), rows_spec(nb, D), rows_spec(na, 1), rows_spec(nb, 1),
            pl.BlockSpec((None, None, D, FT), lambda e, f: (l, e, 0, f)),
            pl.BlockSpec((None, None, D, FT), lambda e, f: (l, e, 0, f)),
            pl.BlockSpec((None, None, FT, D), lambda e, f: (l, e, f, 0)),
        ],
        out_specs=[rows_spec(na, D), rows_spec(nb, D)],
        out_shape=[jax.ShapeDtypeStruct((E, na, D), BF16), jax.ShapeDtypeStruct((E, nb, D), BF16)],
        scratch_shapes=[pltpu.VMEM((na + nb, D), F32)],
        compiler_params=_cparams(2),
        name="moe_experts",
    )(xa, xb, ga, gb, w1, w3, w2)


def _scatter_kernel(x_ref, y_ref, slot_ref, mod_ref, *rest, cap):
    o_ref = rest[-1]
    seqs, _, t = slot_ref.shape
    n = N_EXPERTS * cap
    er = lax.broadcasted_iota(jnp.int32, (N_EXPERTS, n), 0)
    ec = lax.broadcasted_iota(jnp.int32, (N_EXPERTS, n), 1)
    expand = jnp.where(ec // cap == er, 1.0, 0.0).astype(BF16)
    want = (lax.broadcasted_iota(jnp.int32, (t, n), 1) % cap).astype(F32)
    for sq in range(seqs):
        rows = slice(sq * t, (sq + 1) * t)
        slot_wide = _dot_tn(slot_ref[sq].astype(BF16), expand)
        pt = jnp.where(slot_wide == want, 1.0, 0.0).astype(BF16)
        y = jnp.concatenate([y_ref[e, sq * cap:(sq + 1) * cap, :] for e in range(N_EXPERTS)], axis=0)
        x2 = x_ref[rows, :] + mod_ref[5] * _dot(pt, y)
        o_ref[rows, :] = x2 if len(rest) == 1 else _rms(x2, rest[0][...])


def _scatter(x1, y, slot, mods, l, *, t, cap, mod_row0, rows_per_mod, final_g=None):
    R, D = x1.shape
    seqs = _seqs_per_step(t)
    assert seqs == 1 or rows_per_mod is None
    in_specs = [
        pl.BlockSpec((seqs * t, D), lambda b: (b, 0)),
        pl.BlockSpec((N_EXPERTS, seqs * cap, D), lambda b: (0, b, 0)),
        pl.BlockSpec((seqs, N_EXPERTS, t), lambda b: (b, 0, 0)),
        _mod_spec(l, mod_row0, rows_per_mod),
    ]
    args = [x1, y, slot, mods]
    if final_g is not None:
        in_specs.append(pl.BlockSpec((1, D), lambda b: (0, 0)))
        args.append(final_g)
    return pl.pallas_call(
        functools.partial(_scatter_kernel, cap=cap),
        grid=(R // (seqs * t),),
        in_specs=in_specs,
        out_specs=pl.BlockSpec((seqs * t, D), lambda b: (b, 0)),
        out_shape=jax.ShapeDtypeStruct((R, D), F32),
        compiler_params=_cparams(1),
        name="moe_scatter",
    )(*args)


def _stack_kernel(*refs):
    ins, (ckv_ref, kr_ref, st_ref) = refs[:-3], refs[-3:]
    n = len(ins) // 3
    for l in range(n):
        ckv_ref[:, l] = ins[l][...]
        kr_ref[:, l] = ins[n + l][:, :, KR_OFF:KR_OFF + D_ROPE]
        st_ref[:, l] = ins[2 * n + l][...]


def _stack_layers(ckv_l, krp_l, st_l):
    n = len(ckv_l)
    B, T, _ = ckv_l[0].shape
    sb = STACK_SEQS

    def spec(shape):
        zeros = (0,) * (len(shape) - 1)
        return pl.BlockSpec((sb,) + tuple(shape[1:]), lambda b: (b,) + zeros)

    def out_spec(shape):
        zeros = (0,) * len(shape[1:])
        return pl.BlockSpec((sb, n) + tuple(shape[1:]), lambda b: (b, 0) + zeros)

    st_shape = st_l[0].shape
    return pl.pallas_call(
        _stack_kernel,
        grid=(B // sb,),
        in_specs=[spec(a.shape) for a in (*ckv_l, *krp_l, *st_l)],
        out_specs=[out_spec((B, T, KV_RANK)), out_spec((B, T, D_ROPE)), out_spec(st_shape)],
        out_shape=[
            jax.ShapeDtypeStruct((B, n, T, KV_RANK), F32),
            jax.ShapeDtypeStruct((B, n, T, D_ROPE), F32),
            jax.ShapeDtypeStruct((B, n) + tuple(st_shape[1:]), F32),
        ],
        compiler_params=_cparams(1),
        name="stack_layers",
    )(*ckv_l, *krp_l, *st_l)


def _rope_tables(t_lat):
    half = D_ROPE // 2
    nfreq = half // 2
    pos = np.arange(t_lat)
    inv = 1.0 / (ROPE_BASE ** (np.arange(0, half, 2, dtype=np.float64) / half))
    cos = np.ones((t_lat, HEAD_PAD))
    s_up = np.zeros((t_lat, HEAD_PAD))
    s_dn = np.zeros((t_lat, HEAD_PAD))
    for part, p in enumerate((pos // GRID_W, pos % GRID_W)):
        ang = p[:, None].astype(np.float64) * inv
        o = KR_OFF + part * half
        cos[:, o:o + nfreq] = np.cos(ang)
        cos[:, o + nfreq:o + half] = np.cos(ang)
        s_up[:, o:o + nfreq] = -np.sin(ang)
        s_dn[:, o + nfreq:o + half] = np.sin(ang)
    tab = np.stack([cos, s_up, s_dn], axis=0).reshape(3, t_lat // ROW_TILE, ROW_TILE, HEAD_PAD)
    return jnp.asarray(np.moveaxis(tab, 1, 0), dtype=F32)


def _dft_mats(n, scale):
    k = np.arange(n)
    ang = (2.0 * np.pi / n) * ((k[:, None] * k[None, :]) % n)
    return np.cos(ang) * scale, np.sin(ang) * scale


def _dft_tables(t):
    cw, sw = _dft_mats(FFT_GROUP_W, FFT_GROUP_W ** -0.5)
    eye = np.eye(FFT_GROUPS)
    ct, st = _dft_mats(t, t ** -0.5)

    def const(a):
        return jnp.asarray(a, dtype=F32).astype(BF16)

    return const(np.kron(eye, cw)), const(np.kron(eye, sw)), const(np.concatenate([ct, -st], axis=1))


def _layout_w_in(w_in):
    L, D, _ = w_in.shape
    sizes = (H_MLA * (D_NOPE + D_ROPE), KV_RANK, D_ROPE, H_GLA * GLA_DK, H_GLA * GLA_DK, H_GLA * GLA_DV,
             H_GLA * GLA_DV, 2 * GLA_GATE_RANK, FFT_WIDTH, 3 * D_MODEL)
    offs = np.concatenate([[0], np.cumsum(sizes)])
    seg = [w_in[:, :, offs[i]:offs[i + 1]] for i in range(len(sizes))]
    q, ckv, kr, gq, gk, gv, gr, glr, uf, mg = seg
    q = q.reshape(L, D, H_MLA, D_NOPE + D_ROPE)
    q = jnp.pad(q, ((0, 0), (0, 0), (0, 0), (0, HEAD_PAD - D_NOPE - D_ROPE))).reshape(L, D, H_MLA * HEAD_PAD)
    kr = jnp.pad(kr, ((0, 0), (0, 0), (KR_OFF, HEAD_PAD - KR_OFF - D_ROPE)))
    glr = jnp.pad(glr, ((0, 0), (0, 0), (0, LANE - 2 * GLA_GATE_RANK)))
    cols = jnp.concatenate([q, ckv, kr, glr, uf, gq, gk, gv, gr, mg], axis=-1).astype(BF16)
    return jnp.moveaxis(cols.reshape(L, D, N_CHUNK, D), 2, 1)


def _layout_w_ukv(w_ukv):
    L = w_ukv.shape[0]
    w = w_ukv.reshape(L, KV_RANK, H_MLA, D_NOPE + D_V)
    k = jnp.pad(w[..., :D_NOPE], ((0, 0), (0, 0), (0, 0), (0, HEAD_PAD - D_NOPE)))
    v = w[..., D_NOPE:].reshape(L, KV_RANK, H_MLA // 2, 2, D_V)
    z = jnp.zeros_like(v[:, :, :, 0])
    v_even = jnp.concatenate([v[:, :, :, 0], z], axis=-1)
    v_odd = jnp.concatenate([z, v[:, :, :, 1]], axis=-1)
    v2 = jnp.stack([v_even, v_odd], axis=3)
    return (k.reshape(L, KV_RANK, H_MLA * HEAD_PAD).astype(BF16),
            v2.reshape(L, KV_RANK, H_MLA * HEAD_PAD).astype(BF16))


def _layout_gate(gate_w, gate_b):
    L = gate_w.shape[0]
    HW = H_GLA * GLA_DK
    wg = jnp.zeros((L, LANE, 2 * HW), F32)
    for d in range(2):
        wg = wg.at[:, d * GLA_GATE_RANK:(d + 1) * GLA_GATE_RANK, d * HW:(d + 1) * HW].set(gate_w[:, d])
    return wg.astype(BF16), gate_b.reshape(L, 1, 2 * HW)


def kernel(x_prompt, x_sample, cache_ckv, cache_krope, state_gla, c, c_ctx, ada_w, ada_b, norm1_g, norm2_g, w_in, mla_kv_norm_g, mla_w_ukv, gla_gate_w, gla_gate_b, gla_norm_g, w_mla_out, w_gla_out, w_fft_out, w_o, router_w, exp_w1, exp_w3, exp_w2, final_norm_g):
    nb_c, t_c, D = x_prompt.shape
    nb_l, t_l, _ = x_sample.shape
    assert (nb_c * t_c) % ROW_TILE == 0 and ROW_TILE % t_c == 0 and t_l % ROW_TILE == 0 and 1 + nb_l <= MOD_ROWS
    cap_c = CAPACITY_FACTOR * t_c // N_EXPERTS
    cap_l = CAPACITY_FACTOR * t_l // N_EXPERTS

    c_all = jnp.zeros((MOD_ROWS, D), F32).at[0].set(c_ctx).at[1:1 + nb_l].set(c)
    mods = _mods(c_all, ada_w, ada_b).reshape(DEPTH, N_MOD, MOD_ROWS, 1, D)
    rope_tab = _rope_tables(t_l)
    dft_c = _dft_tables(t_c)
    dft_l = _dft_tables(t_l)
    w_in_r = _layout_w_in(w_in)
    wuk, wuv = _layout_w_ukv(mla_w_ukv)
    wgate, bgate = _layout_gate(gla_gate_w, gla_gate_b)
    n1 = norm1_g.reshape(DEPTH, 1, D)
    n2 = norm2_g.reshape(DEPTH, 1, D)
    kvg = mla_kv_norm_g.reshape(DEPTH, 1, KV_RANK)
    glg = gla_norm_g.reshape(DEPTH, 1, GLA_DV)
    wm = w_mla_out.astype(BF16)
    wgo = w_gla_out.astype(BF16)
    wfo = w_fft_out.astype(BF16)
    wo = w_o.astype(BF16)
    rw_hi = router_w.astype(BF16)
    rw_lo = (router_w - rw_hi.astype(F32)).astype(BF16)
    rwt = jnp.zeros((DEPTH, D, 2 * LANE), BF16).at[:, :, :N_EXPERTS].set(rw_hi)
    rwt = rwt.at[:, :, LANE:LANE + N_EXPERTS].set(rw_lo)
    ckr_pad = jnp.pad(cache_krope, ((0, 0), (0, 0), (0, 0), (KR_OFF, HEAD_PAD - KR_OFF - D_ROPE)))

    ctx = dict(t=t_c, cap=cap_c, mod_row0=0, rows_per_mod=None)
    lat = dict(t=t_l, cap=cap_l, mod_row0=1, rows_per_mod=t_l // ROW_TILE)
    xs = [x_prompt.reshape(nb_c * t_c, D), x_sample.reshape(nb_l * t_l, D)]
    ckv_out, kr_out, st_out = [], [], []

    def pre_experts(x, grp, l, is_ctx):
        t, cap = grp["t"], grp["cap"]
        mod_kw = dict(mod_row0=grp["mod_row0"], rows_per_mod=grp["rows_per_mod"])
        proj, ckv, krp = _inproj(x, mods, n1, kvg, w_in_r, l, rope_tab=None if is_ctx else rope_tab, **mod_kw)
        if is_ctx:
            ckv_out.append(ckv.reshape(nb_c, t_c, KV_RANK))
            kr_out.append(krp.reshape(nb_c, t_c, HEAD_PAD))
            o_mla = _attn(proj, ckv, krp, wuk, wuv, l, t_q=t)
            o_gla, st = _gla(proj, wgate, bgate, glg, l, t=t, want_state=True)
            st_out.append(st)
            o_fft = _fft(proj, *dft_c, t=t)
        else:
            o_mla = _attn(proj, ckv, krp, wuk, wuv, l, t_q=t, cache=(cache_ckv, ckr_pad))
            (o_gla,) = _gla(proj, wgate, bgate, glg, l, t=t, state=state_gla)
            o_fft = _fft(proj, *dft_l, t=t)
        rows_per_mod_mix = grp["rows_per_mod"]
        x1, h2, aff = _mix(x, o_mla, o_gla, o_fft, proj, mods, n2, wm, wgo, wfo, wo, rwt, l, t=t,
                           mod_row0=grp["mod_row0"], rows_per_mod=rows_per_mod_mix)
        nseq = aff.shape[0]
        slot = _route(aff.reshape(nseq * N_EXPERTS, t), cap).reshape(nseq, N_EXPERTS, t)
        xg, g = _gather(h2, slot, aff, t=t, cap=cap)
        return x1, slot, xg, g

    for l in range(DEPTH):
        x1_c, slot_c, xg_c, g_c = pre_experts(xs[0], ctx, l, True)
        x1_l, slot_l, xg_l, g_l = pre_experts(xs[1], lat, l, False)
        y_c, y_l = _experts(xg_c, xg_l, g_c, g_l, exp_w1, exp_w3, exp_w2, l)
        fg = final_norm_g.reshape(1, D) if l == DEPTH - 1 else None
        xs = [
            _scatter(x1_c, y_c, slot_c, mods, l, t=t_c, cap=cap_c, mod_row0=0, rows_per_mod=None, final_g=fg),
            _scatter(x1_l, y_l, slot_l, mods, l, t=t_l, cap=cap_l, mod_row0=1, rows_per_mod=1, final_g=fg),
        ]

    y_prompt = xs[0].reshape(nb_c, t_c, D)
    y_sample = xs[1].reshape(nb_l, t_l, D)
    return (y_prompt, y_sample) + tuple(_stack_layers(ckv_out, kr_out, st_out))
```

```python
import functools
import math

import jax
import jax.numpy as jnp
import numpy as np
from jax import lax
from jax.experimental import pallas as pl
from jax.experimental.pallas import tpu as pltpu

F32 = jnp.float32
BF16 = jnp.bfloat16

D_MODEL = 1024
DEPTH = 4
GRID_W = 64
H_MLA = 8
D_NOPE = 64
D_ROPE = 32
D_V = 64
KV_RANK = 256
ROPE_BASE = 10000.0
MLA_SCALE = (D_NOPE + D_ROPE) ** -0.5
H_GLA = 4
GLA_DK = 128
GLA_DV = 128
GLA_GATE_RANK = 16
GLA_TAU = 16.0
GLA_CHUNK = 64
FFT_GROUPS = 4
FFT_GROUP_W = 128
FFT_WIDTH = FFT_GROUPS * FFT_GROUP_W
N_EXPERTS = 16
D_EXPERT = 2048
CAPACITY_FACTOR = 2
N_MOD = 6
EPS = 1e-6

LANE = 128
ROW_TILE = 512
HEAD_PAD = 128
KR_OFF = D_NOPE
N_CHUNK = 7
MOD_ROWS = 16
MOD_PER_STEP = 2
EXPERT_F_TILE = 512
EXPERT_ROW_TILE = 512
ATTN_Q_TILE = 512
STACK_SEQS = 2
SHORT_SEQ_ROWS = 1024
ROUTE_REFINE_STEPS = 16
Q_PRESCALE = MLA_SCALE * math.log2(math.e)
VMEM_LIMIT = 56 * 1024 * 1024


def _cparams(n_axes):
    return pltpu.CompilerParams(dimension_semantics=("arbitrary",) * n_axes, vmem_limit_bytes=VMEM_LIMIT)


def _dot(a, b):
    return jnp.dot(a, b, preferred_element_type=F32)


def _dot_nt(a, b):
    return lax.dot_general(a, b, (((1,), (1,)), ((), ())), preferred_element_type=F32)


def _dot_tn(a, b):
    return lax.dot_general(a, b, (((0,), (0,)), ((), ())), preferred_element_type=F32)


def _sigmoid(x):
    return 1.0 / (1.0 + jnp.exp(-x))


def _rms(x, g):
    return x * lax.rsqrt(jnp.mean(x * x, axis=-1, keepdims=True) + EPS) * g


def _mods_kernel(c_ref, w_ref, b_ref, o_ref):
    c = c_ref[...]
    s = (c * _sigmoid(c)).astype(BF16)
    m = _dot(s, w_ref[...].astype(BF16))
    for k in range(MOD_PER_STEP):
        o_ref[k] = m[:, k * D_MODEL:(k + 1) * D_MODEL] + b_ref[k]


def _mods(c_all, ada_w, ada_b):
    D = D_MODEL
    return pl.pallas_call(
        _mods_kernel,
        grid=(DEPTH, N_MOD // MOD_PER_STEP),
        in_specs=[
            pl.BlockSpec((MOD_ROWS, D), lambda l, k: (0, 0)),
            pl.BlockSpec((None, D, MOD_PER_STEP * D), lambda l, k: (l, 0, k)),
            pl.BlockSpec((None, MOD_PER_STEP, 1, D), lambda l, k: (l, k, 0, 0)),
        ],
        out_specs=pl.BlockSpec((None, MOD_PER_STEP, MOD_ROWS, D), lambda l, k: (l, k, 0, 0)),
        out_shape=jax.ShapeDtypeStruct((DEPTH, N_MOD, MOD_ROWS, D), F32),
        compiler_params=_cparams(2),
        name="adaln_mods",
    )(c_all, ada_w, ada_b.reshape(DEPTH, N_MOD, 1, D))


def _mod_spec(l, mod_row0, rows_per_mod):
    if rows_per_mod is None:
        return pl.BlockSpec((None, N_MOD, None, 1, D_MODEL), lambda i: (l, 0, mod_row0, 0, 0))
    return pl.BlockSpec((None, N_MOD, None, 1, D_MODEL), lambda i: (l, 0, mod_row0 + i // rows_per_mod, 0, 0))


def _inproj_kernel(*refs, has_rope):
    if has_rope:
        x_ref, mod_ref, n1_ref, kvg_ref, w_ref, rope_ref, proj_ref, ckv_ref, krp_ref = refs
        cos, sin_up, sin_dn = rope_ref[0], rope_ref[1], rope_ref[2]

        def rope(a):
            return a * cos + pltpu.roll(a, LANE - 8, 1) * sin_up + pltpu.roll(a, 8, 1) * sin_dn
    else:
        x_ref, mod_ref, n1_ref, kvg_ref, w_ref, proj_ref, ckv_ref, krp_ref = refs

        def rope(a):
            return a

    x = x_ref[...]
    h = _rms(x, n1_ref[...]) * (1.0 + mod_ref[1]) + mod_ref[0]
    hb = h.astype(BF16)

    acc = _dot(hb, w_ref[0])
    for hd in range(H_MLA):
        sl = slice(hd * HEAD_PAD, (hd + 1) * HEAD_PAD)
        proj_ref[0, :, sl] = (rope(acc[:, sl]) * Q_PRESCALE).astype(BF16)

    acc = _dot(hb, w_ref[1])
    ckv_ref[...] = _rms(acc[:, :KV_RANK], kvg_ref[...])
    krp_ref[...] = rope(acc[:, KV_RANK:KV_RANK + HEAD_PAD])
    proj_ref[1] = acc.astype(BF16)

    for c in (2, 3):
        proj_ref[c] = _dot(hb, w_ref[c]).astype(BF16)
    for c in (4, 5, 6):
        proj_ref[c] = _sigmoid(_dot(hb, w_ref[c])).astype(BF16)


def _inproj(x, mods, n1, kvg, w_in_r, l, *, mod_row0, rows_per_mod, rope_tab=None):
    R, D = x.shape
    in_specs = [
        pl.BlockSpec((ROW_TILE, D), lambda i: (i, 0)),
        _mod_spec(l, mod_row0, rows_per_mod),
        pl.BlockSpec((None, 1, D), lambda i: (l, 0, 0)),
        pl.BlockSpec((None, 1, KV_RANK), lambda i: (l, 0, 0)),
        pl.BlockSpec((None, N_CHUNK, D, D), lambda i: (l, 0, 0, 0), pipeline_mode=pl.Buffered(1)),
    ]
    args = [x, mods, n1, kvg, w_in_r]
    if rope_tab is not None:
        nper = rope_tab.shape[0]
        in_specs.append(pl.BlockSpec((None, 3, ROW_TILE, HEAD_PAD), lambda i: (i % nper, 0, 0, 0)))
        args.append(rope_tab)
    return pl.pallas_call(
        functools.partial(_inproj_kernel, has_rope=rope_tab is not None),
        grid=(R // ROW_TILE,),
        in_specs=in_specs,
        out_specs=[
            pl.BlockSpec((N_CHUNK, ROW_TILE, D), lambda i: (0, i, 0)),
            pl.BlockSpec((ROW_TILE, KV_RANK), lambda i: (i, 0)),
            pl.BlockSpec((ROW_TILE, HEAD_PAD), lambda i: (i, 0)),
        ],
        out_shape=[
            jax.ShapeDtypeStruct((N_CHUNK, R, D), BF16),
            jax.ShapeDtypeStruct((R, KV_RANK), F32),
            jax.ShapeDtypeStruct((R, HEAD_PAD), F32),
        ],
        compiler_params=_cparams(1),
        name="inproj",
    )(*args)


def _attn_kernel(*refs, t_q, seqs, has_cache, q_blk):
    if has_cache:
        q_ref, ckv_ref, krp_ref, cckv_ref, ckr_ref, wuk_ref, wuv_ref, o_ref = refs
        ckv = jnp.concatenate([cckv_ref[...], ckv_ref[...]], axis=0)
        krp = jnp.concatenate([ckr_ref[...], krp_ref[...]], axis=0)
    else:
        q_ref, ckv_ref, krp_ref, wuk_ref, wuv_ref, o_ref = refs
        ckv = ckv_ref[...]
        krp = krp_ref[...]
    t_k = ckv.shape[0] // seqs
    ckvb = ckv.astype(BF16)
    k_all = _dot(ckvb, wuk_ref[...])
    v_all = _dot(ckvb, wuv_ref[...]).astype(BF16)
    for sq in range(seqs):
        keys = slice(sq * t_k, (sq + 1) * t_k)
        for pair in range(H_MLA // 2):
            k_pair = [(k_all[keys, hd * HEAD_PAD:(hd + 1) * HEAD_PAD] + krp[keys]).astype(BF16)
                      for hd in (2 * pair, 2 * pair + 1)]
            for qb in range(t_q // q_blk):
                rows = slice(sq * t_q + qb * q_blk, sq * t_q + (qb + 1) * q_blk)
                o_pair = None
                for j, hd in enumerate((2 * pair, 2 * pair + 1)):
                    sl = slice(hd * HEAD_PAD, (hd + 1) * HEAD_PAD)
                    s = _dot_nt(q_ref[rows, sl], k_pair[j])
                    p = jnp.exp2(s - jnp.max(s, axis=-1, keepdims=True))
                    den = jnp.sum(p, axis=-1, keepdims=True)
                    o_h = _dot(p.astype(BF16), v_all[keys, sl]) / den
                    o_pair = o_h if o_pair is None else o_pair + o_h
                o_ref[rows, pair * LANE:(pair + 1) * LANE] = o_pair.astype(BF16)


def _seqs_per_step(t, has_cache=False):
    return 1 if has_cache else max(1, SHORT_SEQ_ROWS // t)


def _attn(proj, ckv, krp, wuk, wuv, l, *, t_q, cache=None):
    R = ckv.shape[0]
    seqs = _seqs_per_step(t_q, cache is not None)
    t_q_seq, t_q = t_q, t_q * seqs
    in_specs = [
        pl.BlockSpec((None, t_q, D_MODEL), lambda b: (0, b, 0)),
        pl.BlockSpec((t_q, KV_RANK), lambda b: (b, 0)),
        pl.BlockSpec((t_q, HEAD_PAD), lambda b: (b, 0)),
    ]
    args = [proj, ckv, krp]
    if cache is not None:
        cckv, ckr = cache
        past = cckv.shape[2]
        in_specs += [
            pl.BlockSpec((None, None, past, KV_RANK), lambda b: (b, l, 0, 0)),
            pl.BlockSpec((None, None, past, HEAD_PAD), lambda b: (b, l, 0, 0)),
        ]
        args += [cckv, ckr]
    in_specs += [
        pl.BlockSpec((None, KV_RANK, H_MLA * HEAD_PAD), lambda b: (l, 0, 0)),
        pl.BlockSpec((None, KV_RANK, H_MLA * HEAD_PAD), lambda b: (l, 0, 0)),
    ]
    args += [wuk, wuv]
    return pl.pallas_call(
        functools.partial(_attn_kernel, t_q=t_q_seq, seqs=seqs, has_cache=cache is not None,
                          q_blk=min(t_q_seq, ATTN_Q_TILE)),
        grid=(R // t_q,),
        in_specs=in_specs,
        out_specs=pl.BlockSpec((t_q, H_MLA * D_V), lambda b: (b, 0)),
        out_shape=jax.ShapeDtypeStruct((R, H_MLA * D_V), BF16),
        compiler_params=_cparams(1),
        name="mla_attn",
    )(*args)


def _gla_kernel(*refs, t, seqs, has_state, want_state):
    refs = list(refs)
    q_ref, k_ref, v_ref, r_ref, glr_ref, wg_ref, bg_ref, g_ref = refs[:8]
    pos = 8
    s0_ref = None
    if has_state:
        s0_ref = refs[pos]
        pos += 1
    o_ref = refs[pos]
    pos += 1
    so_ref = None
    if want_state:
        so_ref = refs[pos]
        pos += 1
    b_ref, of_ref, ob_ref, st_ref = refs[pos:pos + 4]

    CH = GLA_CHUNK
    BLK = 256
    HW = H_GLA * GLA_DK

    pre = _dot(glr_ref[...], wg_ref[...]) + bg_ref[...]
    la = (jnp.minimum(pre, 0.0) - jnp.log(1.0 + jnp.exp(-jnp.abs(pre)))) * (1.0 / GLA_TAU)
    la_hi = la.astype(BF16)
    la_lo = (la - la_hi.astype(F32)).astype(BF16)
    ri = lax.broadcasted_iota(jnp.int32, (BLK, BLK), 0)
    ci = lax.broadcasted_iota(jnp.int32, (BLK, BLK), 1)
    same = (ri // CH) == (ci // CH)
    tri_f = jnp.where(same & (ci <= ri), 1.0, 0.0).astype(BF16)
    tri_b = jnp.where(same & (ci >= ri), 1.0, 0.0).astype(BF16)
    for blk in range(t // BLK):
        rows = slice(blk * BLK, (blk + 1) * BLK)
        hi, lo = la_hi[rows], la_lo[rows]
        b_ref[rows, :HW] = _dot(tri_f, hi[:, :HW]) + _dot(tri_f, lo[:, :HW])
        b_ref[rows, HW:] = _dot(tri_b, hi[:, HW:]) + _dot(tri_b, lo[:, HW:])

    n_st = 2 * H_GLA
    for sq in range(seqs):
        for h in range(H_GLA):
            for d in range(2):
                if has_state:
                    st_ref[sq * n_st + 2 * h + d] = s0_ref[d, h].T
                else:
                    st_ref[sq * n_st + 2 * h + d] = jnp.zeros((GLA_DV, GLA_DK), F32)

    masks = (same & (ci <= ri), same & (ci >= ri))
    scale = GLA_DK ** -0.5
    CPB = BLK // CH
    zero_chunk = jnp.zeros((CH, GLA_DK), BF16)

    def chunk_diag(x):
        return jnp.concatenate(
            [jnp.concatenate([x[c * CH:(c + 1) * CH] if j == c else zero_chunk for j in range(CPB)], axis=1)
             for c in range(CPB)], axis=0)

    def unit(sq, h, d, blk, vt):
        rows = slice(blk * BLK, (blk + 1) * BLK)
        cols = slice(h * GLA_DK, (h + 1) * GLA_DK)
        b = b_ref[rows, d * HW + h * GLA_DK:d * HW + (h + 1) * GLA_DK]
        last = [b[c * CH + CH - 1:c * CH + CH] if d == 0 else b[c * CH:c * CH + 1] for c in range(CPB)]
        b_last = jnp.concatenate([jnp.broadcast_to(x, (CH, GLA_DK)) for x in last], axis=0)
        q = q_ref[rows, cols].astype(F32) * scale
        k = k_ref[rows, cols].astype(F32)
        q_t = (q * jnp.exp(b)).astype(BF16)
        k_t = (k * jnp.exp(-b)).astype(BF16)
        k_end = (k * jnp.exp(b_last - b)).astype(BF16)
        att = jnp.where(masks[d], _dot_nt(q_t, k_t), 0.0).astype(BF16)
        ds = _dot(vt, chunk_diag(k_end))
        st = st_ref[sq * n_st + 2 * h + d]
        prev = [None] * CPB
        for c in (range(CPB) if d == 0 else range(CPB - 1, -1, -1)):
            prev[c] = st.astype(BF16)
            st = st * jnp.exp(last[c]) + ds[:, c * GLA_DK:(c + 1) * GLA_DK]
        st_ref[sq * n_st + 2 * h + d] = st
        o = _dot_nt(jnp.concatenate([att, chunk_diag(q_t)], axis=1),
                    jnp.concatenate([vt] + prev, axis=1))
        if d == 0:
            of_ref[rows, cols] = o
        else:
            ob_ref[rows, cols] = o

    n_blk = t // seqs // BLK
    for n in range(n_blk):
        for sq in range(seqs):
            for h in range(H_GLA):
                cols = slice(h * GLA_DV, (h + 1) * GLA_DV)
                for d, blk in ((0, sq * n_blk + n), (1, sq * n_blk + n_blk - 1 - n)):
                    vt = v_ref[blk * BLK:(blk + 1) * BLK, cols].astype(F32).T.astype(BF16)
                    unit(sq, h, d, blk, vt)

    if want_state:
        for sq in range(seqs):
            for h in range(H_GLA):
                for d in range(2):
                    so_ref[sq, d, h] = st_ref[sq * n_st + 2 * h + d].T

    g = g_ref[...]
    for h in range(H_GLA):
        cols = slice(h * GLA_DV, (h + 1) * GLA_DV)
        o = _rms(of_ref[:, cols] + ob_ref[:, cols], g)
        r = r_ref[:, cols].astype(F32)
        o_ref[:, cols] = (o * (r * _sigmoid(r))).astype(BF16)


def _gla(proj, glr_w, glr_b, gla_g, l, *, t, state=None, want_state=False):
    R = proj.shape[1]
    HW = H_GLA * GLA_DK
    nseq = R // t
    seqs = _seqs_per_step(t, state is not None)
    t = t * seqs

    def spec(chunk, half):
        return pl.BlockSpec((None, t, HW), lambda b: (chunk, b, half))

    in_specs = [
        spec(2, 0), spec(2, 1), spec(3, 0), spec(3, 1),
        pl.BlockSpec((None, t, LANE), lambda b: (1, b, 3)),
        pl.BlockSpec((None, LANE, 2 * HW), lambda b: (l, 0, 0)),
        pl.BlockSpec((None, 1, 2 * HW), lambda b: (l, 0, 0)),
        pl.BlockSpec((None, 1, GLA_DV), lambda b: (l, 0, 0)),
    ]
    args = [proj, proj, proj, proj, proj, glr_w, glr_b, gla_g]
    if state is not None:
        in_specs.append(pl.BlockSpec((None, None, 2, H_GLA, GLA_DK, GLA_DV), lambda b: (b, l, 0, 0, 0, 0)))
        args.append(state)
    out_specs = [pl.BlockSpec((t, HW), lambda b: (b, 0))]
    out_shape = [jax.ShapeDtypeStruct((R, HW), BF16)]
    if want_state:
        out_specs.append(pl.BlockSpec((seqs, 2, H_GLA, GLA_DK, GLA_DV), lambda b: (b, 0, 0, 0, 0)))
        out_shape.append(jax.ShapeDtypeStruct((nseq, 2, H_GLA, GLA_DK, GLA_DV), F32))
    return pl.pallas_call(
        functools.partial(_gla_kernel, t=t, seqs=seqs, has_state=state is not None, want_state=want_state),
        grid=(nseq // seqs,),
        in_specs=in_specs,
        out_specs=out_specs,
        out_shape=out_shape,
        scratch_shapes=[pltpu.VMEM((t, 2 * HW), F32), pltpu.VMEM((t, HW), F32), pltpu.VMEM((t, HW), F32),
                        pltpu.VMEM((seqs * 2 * H_GLA, GLA_DV, GLA_DK), F32)],
        compiler_params=_cparams(1),
        name="gla",
    )(*args)


def _fft_kernel(u_ref, cw_ref, sw_ref, ct_ref, o_ref, *, t):
    u = u_ref[...]
    a = _dot(u, cw_ref[...]).astype(BF16)
    b = _dot(u, sw_ref[...]).astype(BF16)
    for sq in range(u.shape[0] // t):
        rows = slice(sq * t, (sq + 1) * t)
        ab = jnp.concatenate([a[rows], b[rows]], axis=0)
        o_ref[rows, :] = _dot(ct_ref[...], ab).astype(BF16)


def _fft(proj, cw, sw, ct, *, t):
    R = proj.shape[1]
    W = FFT_WIDTH
    rows = t * _seqs_per_step(t)
    return pl.pallas_call(
        functools.partial(_fft_kernel, t=t),
        grid=(R // rows,),
        in_specs=[
            pl.BlockSpec((None, rows, W), lambda b: (1, b, 1)),
            pl.BlockSpec((W, W), lambda b: (0, 0)),
            pl.BlockSpec((W, W), lambda b: (0, 0)),
            pl.BlockSpec((t, 2 * t), lambda b: (0, 0)),
        ],
        out_specs=pl.BlockSpec((rows, W), lambda b: (b, 0)),
        out_shape=jax.ShapeDtypeStruct((R, W), BF16),
        compiler_params=_cparams(1),
        name="fourier_mix",
    )(proj, cw, sw, ct)


def _mix_kernel(x_ref, om_ref, og_ref, of_ref, g0_ref, g1_ref, g2_ref, mod_ref, n2_ref,
                wm_ref, wg_ref, wf_ref, wo_ref, rw_ref, x1_ref, h2_ref, aff_ref, *, seqs_per_tile):
    ym = _dot(om_ref[...], wm_ref[...])
    yg = _dot(og_ref[...], wg_ref[...])
    yf = _dot(of_ref[...], wf_ref[...])
    merged = g0_ref[...].astype(F32) * ym + g1_ref[...].astype(F32) * yg + g2_ref[...].astype(F32) * yf
    z = _dot(merged.astype(BF16), wo_ref[...])
    x1 = x_ref[...] + mod_ref[2] * z
    x1_ref[...] = x1
    h2 = _rms(x1, n2_ref[...]) * (1.0 + mod_ref[4]) + mod_ref[3]
    h_hi = h2.astype(BF16)
    h2_ref[...] = h_hi
    h_lo = (h2 - h_hi.astype(F32)).astype(BF16)
    lg = _dot(h_hi, rw_ref[...])
    lg = lg[:, :LANE] + lg[:, LANE:] + _dot(h_lo, rw_ref[:, :LANE])
    logits = lg.T[:N_EXPERTS]
    m = jnp.max(logits, axis=0, keepdims=True)
    p = jnp.exp(logits - m)
    aff = p / jnp.sum(p, axis=0, keepdims=True)
    if seqs_per_tile == 0:
        aff_ref[...] = aff
    else:
        t = ROW_TILE // seqs_per_tile
        for s in range(seqs_per_tile):
            aff_ref[s] = aff[:, s * t:(s + 1) * t]


def _mix(x, o_mla, o_gla, o_fft, proj, mods, n2, wm, wg, wf, wo, rwt, l, *, t, mod_row0, rows_per_mod):
    R, D = x.shape
    HW = H_MLA * D_V
    nseq = R // t

    def wspec(k):
        return pl.BlockSpec((None, k, D), lambda i: (l, 0, 0))

    if t <= ROW_TILE:
        seqs_per_tile = ROW_TILE // t
        aff_spec = pl.BlockSpec((seqs_per_tile, N_EXPERTS, t), lambda i: (i, 0, 0))
    else:
        seqs_per_tile = 0
        tiles = t // ROW_TILE
        aff_spec = pl.BlockSpec((None, N_EXPERTS, ROW_TILE), lambda i: (i // tiles, 0, i % tiles))
    return pl.pallas_call(
        functools.partial(_mix_kernel, seqs_per_tile=seqs_per_tile),
        grid=(R // ROW_TILE,),
        in_specs=[
            pl.BlockSpec((ROW_TILE, D), lambda i: (i, 0)),
            pl.BlockSpec((ROW_TILE, HW), lambda i: (i, 0)),
            pl.BlockSpec((ROW_TILE, HW), lambda i: (i, 0)),
            pl.BlockSpec((ROW_TILE, HW), lambda i: (i, 0)),
            pl.BlockSpec((None, ROW_TILE, D), lambda i: (4, i, 0)),
            pl.BlockSpec((None, ROW_TILE, D), lambda i: (5, i, 0)),
            pl.BlockSpec((None, ROW_TILE, D), lambda i: (6, i, 0)),
            _mod_spec(l, mod_row0, rows_per_mod),
            pl.BlockSpec((None, 1, D), lambda i: (l, 0, 0)),
            wspec(HW), wspec(HW), wspec(HW), wspec(D),
            pl.BlockSpec((None, D, 2 * LANE), lambda i: (l, 0, 0)),
        ],
        out_specs=[
            pl.BlockSpec((ROW_TILE, D), lambda i: (i, 0)),
            pl.BlockSpec((ROW_TILE, D), lambda i: (i, 0)),
            aff_spec,
        ],
        out_shape=[
            jax.ShapeDtypeStruct((R, D), F32),
            jax.ShapeDtypeStruct((R, D), BF16),
            jax.ShapeDtypeStruct((nseq, N_EXPERTS, t), F32),
        ],
        compiler_params=_cparams(1),
        name="mix_out",
    )(x, o_mla, o_gla, o_fft, proj, proj, proj, mods, n2, wm, wg, wf, wo, rwt)


def _route_kernel(a_ref, slot_ref, *, cap):
    a = a_ref[...]
    rows, t = a.shape

    def count_ge(x):
        return jnp.sum(jnp.where(a >= x, 1.0, 0.0), axis=1, keepdims=True)

    thr = jnp.zeros((rows, 1), jnp.int32)
    for bit in range(30, -1, -1):
        cand = thr | (1 << bit)
        thr = jnp.where(count_ge(pltpu.bitcast(cand, F32)) >= cap, cand, thr)
    lo = pltpu.bitcast(thr, F32)
    hi = pltpu.bitcast(thr + 1, F32)
    for _ in range(ROUTE_REFINE_STEPS):
        mid = lo + (hi - lo) * 0.5
        keep = count_ge(mid) >= cap
        lo = jnp.where(keep, mid, lo)
        hi = jnp.where(keep, hi, mid)
    gt = a > lo
    eq = a == lo
    need = cap - jnp.sum(jnp.where(gt, 1.0, 0.0), axis=1, keepdims=True)
    ri = lax.broadcasted_iota(jnp.int32, (t, t), 0)
    ci = lax.broadcasted_iota(jnp.int32, (t, t), 1)
    before = jnp.where(ri < ci, 1.0, 0.0).astype(BF16)
    eq_rank = _dot(jnp.where(eq, 1.0, 0.0).astype(BF16), before)
    sel = gt | (eq & (eq_rank < need))
    slot = _dot(jnp.where(sel, 1.0, 0.0).astype(BF16), before)
    slot_ref[...] = jnp.where(sel, slot, -1.0)


def _route(a, cap):
    return pl.pallas_call(
        functools.partial(_route_kernel, cap=cap),
        out_shape=jax.ShapeDtypeStruct(a.shape, F32),
        compiler_params=pltpu.CompilerParams(vmem_limit_bytes=VMEM_LIMIT),
        name="route",
    )(a)


def _gather_kernel(h_ref, slot_ref, aff_ref, xg_ref, g_ref, p_ref, *, cap):
    seqs, _, t = slot_ref.shape
    ci = lax.broadcasted_iota(jnp.int32, (cap, t), 0).astype(F32)
    for sq in range(seqs):
        slots = slice(sq * cap, (sq + 1) * cap)
        for e in range(N_EXPERTS):
            hit = ci == slot_ref[sq, e:e + 1, :]
            p_ref[sq, e * cap:(e + 1) * cap, :] = jnp.where(hit, 1.0, 0.0).astype(BF16)
            g_ref[e, slots, :] = jnp.sum(jnp.where(hit, aff_ref[sq, e:e + 1, :], 0.0), axis=1, keepdims=True)
        xg = _dot(p_ref[sq], h_ref[sq * t:(sq + 1) * t, :])
        for e in range(N_EXPERTS):
            xg_ref[e, slots, :] = xg[e * cap:(e + 1) * cap].astype(BF16)


def _gather(h2, slot, aff, *, t, cap):
    R, D = h2.shape
    nseq = R // t
    seqs = _seqs_per_step(t)
    return pl.pallas_call(
        functools.partial(_gather_kernel, cap=cap),
        grid=(nseq // seqs,),
        in_specs=[
            pl.BlockSpec((seqs * t, D), lambda b: (b, 0)),
            pl.BlockSpec((seqs, N_EXPERTS, t), lambda b: (b, 0, 0)),
            pl.BlockSpec((seqs, N_EXPERTS, t), lambda b: (b, 0, 0)),
        ],
        out_specs=[
            pl.BlockSpec((N_EXPERTS, seqs * cap, D), lambda b: (0, b, 0)),
            pl.BlockSpec((N_EXPERTS, seqs * cap, 1), lambda b: (0, b, 0)),
        ],
        out_shape=[
            jax.ShapeDtypeStruct((N_EXPERTS, nseq * cap, D), BF16),
            jax.ShapeDtypeStruct((N_EXPERTS, nseq * cap, 1), F32),
        ],
        scratch_shapes=[pltpu.VMEM((seqs, N_EXPERTS * cap, t), BF16)],
        compiler_params=_cparams(1),
        name="moe_gather",
    )(h2, slot, aff)


def _expert_kernel(xa_ref, xb_ref, ga_ref, gb_ref, w1_ref, w3_ref, w2_ref, ya_ref, yb_ref, acc_ref):
    f = pl.program_id(1)
    nf = pl.num_programs(1)
    na = xa_ref.shape[0]
    groups = ((xa_ref, ga_ref, ya_ref, 0), (xb_ref, gb_ref, yb_ref, na))

    def body(first, last):
        w1 = w1_ref[...].astype(BF16)
        w3 = w3_ref[...].astype(BF16)
        w2 = w2_ref[...].astype(BF16)
        for x_ref, g_ref, y_ref, base in groups:
            n = x_ref.shape[0]
            sub = min(n, EXPERT_ROW_TILE)
            for r in range(n // sub):
                rows = slice(r * sub, (r + 1) * sub)
                arows = slice(base + r * sub, base + (r + 1) * sub)
                x = x_ref[rows, :]
                a = _dot(x, w1)
                gt = _dot(x, w3)
                hm = (a * _sigmoid(a) * gt).astype(BF16)
                contrib = _dot(hm, w2)
                if not first:
                    contrib = acc_ref[arows, :] + contrib
                if last:
                    y_ref[rows, :] = (contrib * g_ref[rows, :]).astype(BF16)
                else:
                    acc_ref[arows, :] = contrib

    @pl.when(f == 0)
    def _():
        body(True, False)

    @pl.when((f > 0) & (f < nf - 1))
    def _():
        body(False, False)

    @pl.when(f == nf - 1)
    def _():
        body(False, True)


def _experts(xa, xb, ga, gb, w1, w3, w2, l):
    E, na, D = xa.shape
    nb = xb.shape[1]
    FT = EXPERT_F_TILE
    assert D_EXPERT // FT >= 2

    def rows_spec(n, w):
        return pl.BlockSpec((None, n, w), lambda e, f: (e, 0, 0))

    return pl.pallas_call(
        _expert_kernel,
        grid=(E, D_EXPERT // FT),
        in_specs=[
            rows_spec(na, D), rows_spec(nb, D), rows_spec(na, 1), rows_spec(nb, 1),
            pl.BlockSpec((None, None, D, FT), lambda e, f: (l, e, 0, f)),
            pl.BlockSpec((None, None, D, FT), lambda e, f: (l, e, 0, f)),
            pl.BlockSpec((None, None, FT, D), lambda e, f: (l, e, f, 0)),
        ],
        out_specs=[rows_spec(na, D), rows_spec(nb, D)],
        out_shape=[jax.ShapeDtypeStruct((E, na, D), BF16), jax.ShapeDtypeStruct((E, nb, D), BF16)],
        scratch_shapes=[pltpu.VMEM((na + nb, D), F32)],
        compiler_params=_cparams(2),
        name="moe_experts",
    )(xa, xb, ga, gb, w1, w3, w2)


def _scatter_kernel(x_ref, y_ref, slot_ref, mod_ref, *rest, cap):
    o_ref = rest[-1]
    seqs, _, t = slot_ref.shape
    n = N_EXPERTS * cap
    er = lax.broadcasted_iota(jnp.int32, (N_EXPERTS, n), 0)
    ec = lax.broadcasted_iota(jnp.int32, (N_EXPERTS, n), 1)
    expand = jnp.where(ec // cap == er, 1.0, 0.0).astype(BF16)
    want = (lax.broadcasted_iota(jnp.int32, (t, n), 1) % cap).astype(F32)
    for sq in range(seqs):
        rows = slice(sq * t, (sq + 1) * t)
        slot_wide = _dot_tn(slot_ref[sq].astype(BF16), expand)
        pt = jnp.where(slot_wide == want, 1.0, 0.0).astype(BF16)
        y = jnp.concatenate([y_ref[e, sq * cap:(sq + 1) * cap, :] for e in range(N_EXPERTS)], axis=0)
        x2 = x_ref[rows, :] + mod_ref[5] * _dot(pt, y)
        o_ref[rows, :] = x2 if len(rest) == 1 else _rms(x2, rest[0][...])


def _scatter(x1, y, slot, mods, l, *, t, cap, mod_row0, rows_per_mod, final_g=None):
    R, D = x1.shape
    seqs = _seqs_per_step(t)
    assert seqs == 1 or rows_per_mod is None
    in_specs = [
        pl.BlockSpec((seqs * t, D), lambda b: (b, 0)),
        pl.BlockSpec((N_EXPERTS, seqs * cap, D), lambda b: (0, b, 0)),
        pl.BlockSpec((seqs, N_EXPERTS, t), lambda b: (b, 0, 0)),
        _mod_spec(l, mod_row0, rows_per_mod),
    ]
    args = [x1, y, slot, mods]
    if final_g is not None:
        in_specs.append(pl.BlockSpec((1, D), lambda b: (0, 0)))
        args.append(final_g)
    return pl.pallas_call(
        functools.partial(_scatter_kernel, cap=cap),
        grid=(R // (seqs * t),),
        in_specs=in_specs,
        out_specs=pl.BlockSpec((seqs * t, D), lambda b: (b, 0)),
        out_shape=jax.ShapeDtypeStruct((R, D), F32),
        compiler_params=_cparams(1),
        name="moe_scatter",
    )(*args)


def _stack_kernel(*refs):
    ins, (ckv_ref, kr_ref, st_ref) = refs[:-3], refs[-3:]
    n = len(ins) // 3
    for l in range(n):
        ckv_ref[:, l] = ins[l][...]
        kr_ref[:, l] = ins[n + l][:, :, KR_OFF:KR_OFF + D_ROPE]
        st_ref[:, l] = ins[2 * n + l][...]


def _stack_layers(ckv_l, krp_l, st_l):
    n = len(ckv_l)
    B, T, _ = ckv_l[0].shape
    sb = STACK_SEQS

    def spec(shape):
        zeros = (0,) * (len(shape) - 1)
        return pl.BlockSpec((sb,) + tuple(shape[1:]), lambda b: (b,) + zeros)

    def out_spec(shape):
        zeros = (0,) * len(shape[1:])
        return pl.BlockSpec((sb, n) + tuple(shape[1:]), lambda b: (b, 0) + zeros)

    st_shape = st_l[0].shape
    return pl.pallas_call(
        _stack_kernel,
        grid=(B // sb,),
        in_specs=[spec(a.shape) for a in (*ckv_l, *krp_l, *st_l)],
        out_specs=[out_spec((B, T, KV_RANK)), out_spec((B, T, D_ROPE)), out_spec(st_shape)],
        out_shape=[
            jax.ShapeDtypeStruct((B, n, T, KV_RANK), F32),
            jax.ShapeDtypeStruct((B, n, T, D_ROPE), F32),
            jax.ShapeDtypeStruct((B, n) + tuple(st_shape[1:]), F32),
        ],
        compiler_params=_cparams(1),
        name="stack_layers",
    )(*ckv_l, *krp_l, *st_l)


def _rope_tables(t_lat):
    half = D_ROPE // 2
    nfreq = half // 2
    pos = np.arange(t_lat)
    inv = 1.0 / (ROPE_BASE ** (np.arange(0, half, 2, dtype=np.float64) / half))
    cos = np.ones((t_lat, HEAD_PAD))
    s_up = np.zeros((t_lat, HEAD_PAD))
    s_dn = np.zeros((t_lat, HEAD_PAD))
    for part, p in enumerate((pos // GRID_W, pos % GRID_W)):
        ang = p[:, None].astype(np.float64) * inv
        o = KR_OFF + part * half
        cos[:, o:o + nfreq] = np.cos(ang)
        cos[:, o + nfreq:o + half] = np.cos(ang)
        s_up[:, o:o + nfreq] = -np.sin(ang)
        s_dn[:, o + nfreq:o + half] = np.sin(ang)
    tab = np.stack([cos, s_up, s_dn], axis=0).reshape(3, t_lat // ROW_TILE, ROW_TILE, HEAD_PAD)
    return jnp.asarray(np.moveaxis(tab, 1, 0), dtype=F32)


def _dft_mats(n, scale):
    k = np.arange(n)
    ang = (2.0 * np.pi / n) * ((k[:, None] * k[None, :]) % n)
    return np.cos(ang) * scale, np.sin(ang) * scale


def _dft_tables(t):
    cw, sw = _dft_mats(FFT_GROUP_W, FFT_GROUP_W ** -0.5)
    eye = np.eye(FFT_GROUPS)
    ct, st = _dft_mats(t, t ** -0.5)

    def const(a):
        return jnp.asarray(a, dtype=F32).astype(BF16)

    return const(np.kron(eye, cw)), const(np.kron(eye, sw)), const(np.concatenate([ct, -st], axis=1))


def _layout_w_in(w_in):
    L, D, _ = w_in.shape
    sizes = (H_MLA * (D_NOPE + D_ROPE), KV_RANK, D_ROPE, H_GLA * GLA_DK, H_GLA * GLA_DK, H_GLA * GLA_DV,
             H_GLA * GLA_DV, 2 * GLA_GATE_RANK, FFT_WIDTH, 3 * D_MODEL)
    offs = np.concatenate([[0], np.cumsum(sizes)])
    seg = [w_in[:, :, offs[i]:offs[i + 1]].astype(BF16) for i in range(len(sizes))]
    q, ckv, kr, gq, gk, gv, gr, glr, uf, mg = seg
    q = q.reshape(L, D, H_MLA, D_NOPE + D_ROPE)
    q = jnp.pad(q, ((0, 0), (0, 0), (0, 0), (0, HEAD_PAD - D_NOPE - D_ROPE))).reshape(L, D, H_MLA * HEAD_PAD)
    kr = jnp.pad(kr, ((0, 0), (0, 0), (KR_OFF, HEAD_PAD - KR_OFF - D_ROPE)))
    glr = jnp.pad(glr, ((0, 0), (0, 0), (0, LANE - 2 * GLA_GATE_RANK)))
    chunks = [q, jnp.concatenate([ckv, kr, glr, uf], axis=-1), jnp.concatenate([gq, gk], axis=-1),
              jnp.concatenate([gv, gr], axis=-1)] + [mg[:, :, k * D:(k + 1) * D] for k in range(3)]
    assert len(chunks) == N_CHUNK and all(c.shape == (L, D, D) for c in chunks)
    return jnp.stack(chunks, axis=1)


def _layout_w_ukv(w_ukv):
    L = w_ukv.shape[0]
    w = w_ukv.reshape(L, KV_RANK, H_MLA, D_NOPE + D_V)
    k = jnp.pad(w[..., :D_NOPE], ((0, 0), (0, 0), (0, 0), (0, HEAD_PAD - D_NOPE)))
    v = w[..., D_NOPE:].reshape(L, KV_RANK, H_MLA // 2, 2, D_V)
    z = jnp.zeros_like(v[:, :, :, 0])
    v_even = jnp.concatenate([v[:, :, :, 0], z], axis=-1)
    v_odd = jnp.concatenate([z, v[:, :, :, 1]], axis=-1)
    v2 = jnp.stack([v_even, v_odd], axis=3)
    return (k.reshape(L, KV_RANK, H_MLA * HEAD_PAD).astype(BF16),
            v2.reshape(L, KV_RANK, H_MLA * HEAD_PAD).astype(BF16))


def _layout_gate(gate_w, gate_b):
    L = gate_w.shape[0]
    HW = H_GLA * GLA_DK
    wg = jnp.zeros((L, LANE, 2 * HW), F32)
    for d in range(2):
        wg = wg.at[:, d * GLA_GATE_RANK:(d + 1) * GLA_GATE_RANK, d * HW:(d + 1) * HW].set(gate_w[:, d])
    return wg.astype(BF16), gate_b.reshape(L, 1, 2 * HW)


def kernel(x_prompt, x_sample, cache_ckv, cache_krope, state_gla, c, c_ctx, ada_w, ada_b, norm1_g, norm2_g, w_in, mla_kv_norm_g, mla_w_ukv, gla_gate_w, gla_gate_b, gla_norm_g, w_mla_out, w_gla_out, w_fft_out, w_o, router_w, exp_w1, exp_w3, exp_w2, final_norm_g):
    nb_c, t_c, D = x_prompt.shape
    nb_l, t_l, _ = x_sample.shape
    assert (nb_c * t_c) % ROW_TILE == 0 and ROW_TILE % t_c == 0 and t_l % ROW_TILE == 0 and 1 + nb_l <= MOD_ROWS
    cap_c = CAPACITY_FACTOR * t_c // N_EXPERTS
    cap_l = CAPACITY_FACTOR * t_l // N_EXPERTS

    c_all = jnp.zeros((MOD_ROWS, D), F32).at[0].set(c_ctx).at[1:1 + nb_l].set(c)
    mods = _mods(c_all, ada_w, ada_b).reshape(DEPTH, N_MOD, MOD_ROWS, 1, D)
    rope_tab = _rope_tables(t_l)
    dft_c = _dft_tables(t_c)
    dft_l = _dft_tables(t_l)
    w_in_r = _layout_w_in(w_in)
    wuk, wuv = _layout_w_ukv(mla_w_ukv)
    wgate, bgate = _layout_gate(gla_gate_w, gla_gate_b)
    n1 = norm1_g.reshape(DEPTH, 1, D)
    n2 = norm2_g.reshape(DEPTH, 1, D)
    kvg = mla_kv_norm_g.reshape(DEPTH, 1, KV_RANK)
    glg = gla_norm_g.reshape(DEPTH, 1, GLA_DV)
    wm = w_mla_out.astype(BF16)
    wgo = w_gla_out.astype(BF16)
    wfo = w_fft_out.astype(BF16)
    wo = w_o.astype(BF16)
    rw_hi = router_w.astype(BF16)
    rw_lo = (router_w - rw_hi.astype(F32)).astype(BF16)
    rwt = jnp.zeros((DEPTH, D, 2 * LANE), BF16).at[:, :, :N_EXPERTS].set(rw_hi)
    rwt = rwt.at[:, :, LANE:LANE + N_EXPERTS].set(rw_lo)
    ckr_pad = jnp.pad(cache_krope, ((0, 0), (0, 0), (0, 0), (KR_OFF, HEAD_PAD - KR_OFF - D_ROPE)))

    ctx = dict(t=t_c, cap=cap_c, mod_row0=0, rows_per_mod=None)
    lat = dict(t=t_l, cap=cap_l, mod_row0=1, rows_per_mod=t_l // ROW_TILE)
    xs = [x_prompt.reshape(nb_c * t_c, D), x_sample.reshape(nb_l * t_l, D)]
    ckv_out, kr_out, st_out = [], [], []

    def pre_experts(x, grp, l, is_ctx):
        t, cap = grp["t"], grp["cap"]
        mod_kw = dict(mod_row0=grp["mod_row0"], rows_per_mod=grp["rows_per_mod"])
        proj, ckv, krp = _inproj(x, mods, n1, kvg, w_in_r, l, rope_tab=None if is_ctx else rope_tab, **mod_kw)
        if is_ctx:
            ckv_out.append(ckv.reshape(nb_c, t_c, KV_RANK))
            kr_out.append(krp.reshape(nb_c, t_c, HEAD_PAD))
            o_mla = _attn(proj, ckv, krp, wuk, wuv, l, t_q=t)
            o_gla, st = _gla(proj, wgate, bgate, glg, l, t=t, want_state=True)
            st_out.append(st)
            o_fft = _fft(proj, *dft_c, t=t)
        else:
            o_mla = _attn(proj, ckv, krp, wuk, wuv, l, t_q=t, cache=(cache_ckv, ckr_pad))
            (o_gla,) = _gla(proj, wgate, bgate, glg, l, t=t, state=state_gla)
            o_fft = _fft(proj, *dft_l, t=t)
        rows_per_mod_mix = grp["rows_per_mod"]
        x1, h2, aff = _mix(x, o_mla, o_gla, o_fft, proj, mods, n2, wm, wgo, wfo, wo, rwt, l, t=t,
                           mod_row0=grp["mod_row0"], rows_per_mod=rows_per_mod_mix)
        nseq = aff.shape[0]
        slot = _route(aff.reshape(nseq * N_EXPERTS, t), cap).reshape(nseq, N_EXPERTS, t)
        xg, g = _gather(h2, slot, aff, t=t, cap=cap)
        return x1, slot, xg, g

    for l in range(DEPTH):
        x1_c, slot_c, xg_c, g_c = pre_experts(xs[0], ctx, l, True)
        x1_l, slot_l, xg_l, g_l = pre_experts(xs[1], lat, l, False)
        y_c, y_l = _experts(xg_c, xg_l, g_c, g_l, exp_w1, exp_w3, exp_w2, l)
        fg = final_norm_g.reshape(1, D) if l == DEPTH - 1 else None
        xs = [
            _scatter(x1_c, y_c, slot_c, mods, l, t=t_c, cap=cap_c, mod_row0=0, rows_per_mod=None, final_g=fg),
            _scatter(x1_l, y_l, slot_l, mods, l, t=t_l, cap=cap_l, mod_row0=1, rows_per_mod=1, final_g=fg),
        ]

    y_prompt = xs[0].reshape(nb_c, t_c, D)
    y_sample = xs[1].reshape(nb_l, t_l, D)
    return (y_prompt, y_sample) + tuple(_stack_layers(ckv_out, kr_out, st_out))
```

```python
import functools
import math

import jax
import jax.numpy as jnp
import numpy as np
from jax import lax
from jax.experimental import pallas as pl
from jax.experimental.pallas import tpu as pltpu

F32 = jnp.float32
BF16 = jnp.bfloat16

D_MODEL = 1024
DEPTH = 4
GRID_W = 64
H_MLA = 8
D_NOPE = 64
D_ROPE = 32
D_V = 64
KV_RANK = 256
ROPE_BASE = 10000.0
MLA_SCALE = (D_NOPE + D_ROPE) ** -0.5
H_GLA = 4
GLA_DK = 128
GLA_DV = 128
GLA_GATE_RANK = 16
GLA_TAU = 16.0
GLA_CHUNK = 64
FFT_GROUPS = 4
FFT_GROUP_W = 128
FFT_WIDTH = FFT_GROUPS * FFT_GROUP_W
N_EXPERTS = 16
D_EXPERT = 2048
CAPACITY_FACTOR = 2
N_MOD = 6
EPS = 1e-6

LANE = 128
ROW_TILE = 512
HEAD_PAD = 128
KR_OFF = D_NOPE
N_CHUNK = 7
MOD_ROWS = 16
MOD_PER_STEP = 2
EXPERT_F_TILE = 512
EXPERT_ROW_TILE = 512
ATTN_Q_TILE = 512
STACK_SEQS = 2
SHORT_SEQ_ROWS = 1024
ROUTE_REFINE_STEPS = 16
Q_PRESCALE = MLA_SCALE * math.log2(math.e)
VMEM_LIMIT = 56 * 1024 * 1024


def _cparams(n_axes):
    return pltpu.CompilerParams(dimension_semantics=("arbitrary",) * n_axes, vmem_limit_bytes=VMEM_LIMIT)


def _dot(a, b):
    return jnp.dot(a, b, preferred_element_type=F32)


def _dot_nt(a, b):
    return lax.dot_general(a, b, (((1,), (1,)), ((), ())), preferred_element_type=F32)


def _dot_tn(a, b):
    return lax.dot_general(a, b, (((0,), (0,)), ((), ())), preferred_element_type=F32)


def _sigmoid(x):
    return 1.0 / (1.0 + jnp.exp(-x))


def _rms(x, g):
    return x * lax.rsqrt(jnp.mean(x * x, axis=-1, keepdims=True) + EPS) * g


def _mods_kernel(c_ref, w_ref, b_ref, o_ref):
    c = c_ref[...]
    s = (c * _sigmoid(c)).astype(BF16)
    m = _dot(s, w_ref[...].astype(BF16))
    for k in range(MOD_PER_STEP):
        o_ref[k] = m[:, k * D_MODEL:(k + 1) * D_MODEL] + b_ref[k]


def _mods(c_all, ada_w, ada_b):
    D = D_MODEL
    return pl.pallas_call(
        _mods_kernel,
        grid=(DEPTH, N_MOD // MOD_PER_STEP),
        in_specs=[
            pl.BlockSpec((MOD_ROWS, D), lambda l, k: (0, 0)),
            pl.BlockSpec((None, D, MOD_PER_STEP * D), lambda l, k: (l, 0, k)),
            pl.BlockSpec((None, MOD_PER_STEP, 1, D), lambda l, k: (l, k, 0, 0)),
        ],
        out_specs=pl.BlockSpec((None, MOD_PER_STEP, MOD_ROWS, D), lambda l, k: (l, k, 0, 0)),
        out_shape=jax.ShapeDtypeStruct((DEPTH, N_MOD, MOD_ROWS, D), F32),
        compiler_params=_cparams(2),
        name="adaln_mods",
    )(c_all, ada_w, ada_b.reshape(DEPTH, N_MOD, 1, D))


def _mod_spec(l, mod_row0, rows_per_mod):
    if rows_per_mod is None:
        return pl.BlockSpec((None, N_MOD, None, 1, D_MODEL), lambda i: (l, 0, mod_row0, 0, 0))
    return pl.BlockSpec((None, N_MOD, None, 1, D_MODEL), lambda i: (l, 0, mod_row0 + i // rows_per_mod, 0, 0))


def _inproj_kernel(*refs, has_rope):
    if has_rope:
        x_ref, mod_ref, n1_ref, kvg_ref, w_ref, rope_ref, proj_ref, ckv_ref, krp_ref = refs
        cos, sin_up, sin_dn = rope_ref[0], rope_ref[1], rope_ref[2]

        def rope(a):
            return a * cos + pltpu.roll(a, LANE - 8, 1) * sin_up + pltpu.roll(a, 8, 1) * sin_dn
    else:
        x_ref, mod_ref, n1_ref, kvg_ref, w_ref, proj_ref, ckv_ref, krp_ref = refs

        def rope(a):
            return a

    x = x_ref[...]
    h = _rms(x, n1_ref[...]) * (1.0 + mod_ref[1]) + mod_ref[0]
    hb = h.astype(BF16)

    acc = _dot(hb, w_ref[0])
    for hd in range(H_MLA):
        sl = slice(hd * HEAD_PAD, (hd + 1) * HEAD_PAD)
        proj_ref[0, :, sl] = (rope(acc[:, sl]) * Q_PRESCALE).astype(BF16)

    acc = _dot(hb, w_ref[1])
    ckv_ref[...] = _rms(acc[:, :KV_RANK], kvg_ref[...])
    krp_ref[...] = rope(acc[:, KV_RANK:KV_RANK + HEAD_PAD])
    proj_ref[1] = acc.astype(BF16)

    for c in (2, 3):
        proj_ref[c] = _dot(hb, w_ref[c]).astype(BF16)
    for c in (4, 5, 6):
        proj_ref[c] = _sigmoid(_dot(hb, w_ref[c])).astype(BF16)


def _inproj(x, mods, n1, kvg, w_in_r, l, *, mod_row0, rows_per_mod, rope_tab=None):
    R, D = x.shape
    in_specs = [
        pl.BlockSpec((ROW_TILE, D), lambda i: (i, 0)),
        _mod_spec(l, mod_row0, rows_per_mod),
        pl.BlockSpec((None, 1, D), lambda i: (l, 0, 0)),
        pl.BlockSpec((None, 1, KV_RANK), lambda i: (l, 0, 0)),
        pl.BlockSpec((None, N_CHUNK, D, D), lambda i: (l, 0, 0, 0), pipeline_mode=pl.Buffered(1)),
    ]
    args = [x, mods, n1, kvg, w_in_r]
    if rope_tab is not None:
        nper = rope_tab.shape[0]
        in_specs.append(pl.BlockSpec((None, 3, ROW_TILE, HEAD_PAD), lambda i: (i % nper, 0, 0, 0)))
        args.append(rope_tab)
    return pl.pallas_call(
        functools.partial(_inproj_kernel, has_rope=rope_tab is not None),
        grid=(R // ROW_TILE,),
        in_specs=in_specs,
        out_specs=[
            pl.BlockSpec((N_CHUNK, ROW_TILE, D), lambda i: (0, i, 0)),
            pl.BlockSpec((ROW_TILE, KV_RANK), lambda i: (i, 0)),
            pl.BlockSpec((ROW_TILE, HEAD_PAD), lambda i: (i, 0)),
        ],
        out_shape=[
            jax.ShapeDtypeStruct((N_CHUNK, R, D), BF16),
            jax.ShapeDtypeStruct((R, KV_RANK), F32),
            jax.ShapeDtypeStruct((R, HEAD_PAD), F32),
        ],
        compiler_params=_cparams(1),
        name="inproj",
    )(*args)


def _attn_kernel(*refs, t_q, seqs, has_cache, q_blk):
    if has_cache:
        q_ref, ckv_ref, krp_ref, cckv_ref, ckr_ref, wuk_ref, wuv_ref, o_ref = refs
        ckv = jnp.concatenate([cckv_ref[...], ckv_ref[...]], axis=0)
        krp = jnp.concatenate([ckr_ref[...], krp_ref[...]], axis=0)
    else:
        q_ref, ckv_ref, krp_ref, wuk_ref, wuv_ref, o_ref = refs
        ckv = ckv_ref[...]
        krp = krp_ref[...]
    t_k = ckv.shape[0] // seqs
    ckvb = ckv.astype(BF16)
    k_all = _dot(ckvb, wuk_ref[...])
    v_all = _dot(ckvb, wuv_ref[...]).astype(BF16)
    for sq in range(seqs):
        keys = slice(sq * t_k, (sq + 1) * t_k)
        for pair in range(H_MLA // 2):
            k_pair = [(k_all[keys, hd * HEAD_PAD:(hd + 1) * HEAD_PAD] + krp[keys]).astype(BF16)
                      for hd in (2 * pair, 2 * pair + 1)]
            for qb in range(t_q // q_blk):
                rows = slice(sq * t_q + qb * q_blk, sq * t_q + (qb + 1) * q_blk)
                o_pair = None
                for j, hd in enumerate((2 * pair, 2 * pair + 1)):
                    sl = slice(hd * HEAD_PAD, (hd + 1) * HEAD_PAD)
                    s = _dot_nt(q_ref[rows, sl], k_pair[j])
                    p = jnp.exp2(s - jnp.max(s, axis=-1, keepdims=True))
                    den = jnp.sum(p, axis=-1, keepdims=True)
                    o_h = _dot(p.astype(BF16), v_all[keys, sl]) / den
                    o_pair = o_h if o_pair is None else o_pair + o_h
                o_ref[rows, pair * LANE:(pair + 1) * LANE] = o_pair.astype(BF16)


def _seqs_per_step(t, has_cache=False):
    return 1 if has_cache else max(1, SHORT_SEQ_ROWS // t)


def _attn(proj, ckv, krp, wuk, wuv, l, *, t_q, cache=None):
    R = ckv.shape[0]
    seqs = _seqs_per_step(t_q, cache is not None)
    t_q_seq, t_q = t_q, t_q * seqs
    in_specs = [
        pl.BlockSpec((None, t_q, D_MODEL), lambda b: (0, b, 0)),
        pl.BlockSpec((t_q, KV_RANK), lambda b: (b, 0)),
        pl.BlockSpec((t_q, HEAD_PAD), lambda b: (b, 0)),
    ]
    args = [proj, ckv, krp]
    if cache is not None:
        cckv, ckr = cache
        past = cckv.shape[2]
        in_specs += [
            pl.BlockSpec((None, None, past, KV_RANK), lambda b: (b, l, 0, 0)),
            pl.BlockSpec((None, None, past, HEAD_PAD), lambda b: (b, l, 0, 0)),
        ]
        args += [cckv, ckr]
    in_specs += [
        pl.BlockSpec((None, KV_RANK, H_MLA * HEAD_PAD), lambda b: (l, 0, 0)),
        pl.BlockSpec((None, KV_RANK, H_MLA * HEAD_PAD), lambda b: (l, 0, 0)),
    ]
    args += [wuk, wuv]
    return pl.pallas_call(
        functools.partial(_attn_kernel, t_q=t_q_seq, seqs=seqs, has_cache=cache is not None,
                          q_blk=min(t_q_seq, ATTN_Q_TILE)),
        grid=(R // t_q,),
        in_specs=in_specs,
        out_specs=pl.BlockSpec((t_q, H_MLA * D_V), lambda b: (b, 0)),
        out_shape=jax.ShapeDtypeStruct((R, H_MLA * D_V), BF16),
        compiler_params=_cparams(1),
        name="mla_attn",
    )(*args)


def _gla_kernel(*refs, t, seqs, has_state, want_state):
    refs = list(refs)
    q_ref, k_ref, v_ref, r_ref, glr_ref, wg_ref, bg_ref, g_ref = refs[:8]
    pos = 8
    s0_ref = None
    if has_state:
        s0_ref = refs[pos]
        pos += 1
    o_ref = refs[pos]
    pos += 1
    so_ref = None
    if want_state:
        so_ref = refs[pos]
        pos += 1
    b_ref, of_ref, ob_ref, st_ref = refs[pos:pos + 4]

    CH = GLA_CHUNK
    BLK = 256
    HW = H_GLA * GLA_DK

    pre = _dot(glr_ref[...], wg_ref[...]) + bg_ref[...]
    la = (jnp.minimum(pre, 0.0) - jnp.log(1.0 + jnp.exp(-jnp.abs(pre)))) * (1.0 / GLA_TAU)
    la_hi = la.astype(BF16)
    la_lo = (la - la_hi.astype(F32)).astype(BF16)
    ri = lax.broadcasted_iota(jnp.int32, (BLK, BLK), 0)
    ci = lax.broadcasted_iota(jnp.int32, (BLK, BLK), 1)
    same = (ri // CH) == (ci // CH)
    tri_f = jnp.where(same & (ci <= ri), 1.0, 0.0).astype(BF16)
    tri_b = jnp.where(same & (ci >= ri), 1.0, 0.0).astype(BF16)
    for blk in range(t // BLK):
        rows = slice(blk * BLK, (blk + 1) * BLK)
        hi, lo = la_hi[rows], la_lo[rows]
        b_ref[rows, :HW] = _dot(tri_f, hi[:, :HW]) + _dot(tri_f, lo[:, :HW])
        b_ref[rows, HW:] = _dot(tri_b, hi[:, HW:]) + _dot(tri_b, lo[:, HW:])

    n_st = 2 * H_GLA
    for sq in range(seqs):
        for h in range(H_GLA):
            for d in range(2):
                if has_state:
                    st_ref[sq * n_st + 2 * h + d] = s0_ref[d, h].T
                else:
                    st_ref[sq * n_st + 2 * h + d] = jnp.zeros((GLA_DV, GLA_DK), F32)

    masks = (same & (ci <= ri), same & (ci >= ri))
    scale = GLA_DK ** -0.5
    CPB = BLK // CH
    zero_chunk = jnp.zeros((CH, GLA_DK), BF16)

    def chunk_diag(x):
        return jnp.concatenate(
            [jnp.concatenate([x[c * CH:(c + 1) * CH] if j == c else zero_chunk for j in range(CPB)], axis=1)
             for c in range(CPB)], axis=0)

    def unit(sq, h, d, blk, vt):
        rows = slice(blk * BLK, (blk + 1) * BLK)
        cols = slice(h * GLA_DK, (h + 1) * GLA_DK)
        b = b_ref[rows, d * HW + h * GLA_DK:d * HW + (h + 1) * GLA_DK]
        last = [b[c * CH + CH - 1:c * CH + CH] if d == 0 else b[c * CH:c * CH + 1] for c in range(CPB)]
        b_last = jnp.concatenate([jnp.broadcast_to(x, (CH, GLA_DK)) for x in last], axis=0)
        q = q_ref[rows, cols].astype(F32) * scale
        k = k_ref[rows, cols].astype(F32)
        q_t = (q * jnp.exp(b)).astype(BF16)
        k_t = (k * jnp.exp(-b)).astype(BF16)
        k_end = (k * jnp.exp(b_last - b)).astype(BF16)
        att = jnp.where(masks[d], _dot_nt(q_t, k_t), 0.0).astype(BF16)
        ds = _dot(vt, chunk_diag(k_end))
        st = st_ref[sq * n_st + 2 * h + d]
        prev = [None] * CPB
        for c in (range(CPB) if d == 0 else range(CPB - 1, -1, -1)):
            prev[c] = st.astype(BF16)
            st = st * jnp.exp(last[c]) + ds[:, c * GLA_DK:(c + 1) * GLA_DK]
        st_ref[sq * n_st + 2 * h + d] = st
        o = _dot_nt(jnp.concatenate([att, chunk_diag(q_t)], axis=1),
                    jnp.concatenate([vt] + prev, axis=1))
        if d == 0:
            of_ref[rows, cols] = o
        else:
            ob_ref[rows, cols] = o

    n_blk = t // seqs // BLK
    for n in range(n_blk):
        for sq in range(seqs):
            for h in range(H_GLA):
                cols = slice(h * GLA_DV, (h + 1) * GLA_DV)
                for d, blk in ((0, sq * n_blk + n), (1, sq * n_blk + n_blk - 1 - n)):
                    vt = v_ref[blk * BLK:(blk + 1) * BLK, cols].astype(F32).T.astype(BF16)
                    unit(sq, h, d, blk, vt)

    if want_state:
        for sq in range(seqs):
            for h in range(H_GLA):
                for d in range(2):
                    so_ref[sq, d, h] = st_ref[sq * n_st + 2 * h + d].T

    g = g_ref[...]
    for h in range(H_GLA):
        cols = slice(h * GLA_DV, (h + 1) * GLA_DV)
        o = _rms(of_ref[:, cols] + ob_ref[:, cols], g)
        r = r_ref[:, cols].astype(F32)
        o_ref[:, cols] = (o * (r * _sigmoid(r))).astype(BF16)


def _gla(proj, glr_w, glr_b, gla_g, l, *, t, state=None, want_state=False):
    R = proj.shape[1]
    HW = H_GLA * GLA_DK
    nseq = R // t
    seqs = _seqs_per_step(t, state is not None)
    t = t * seqs

    def spec(chunk, half):
        return pl.BlockSpec((None, t, HW), lambda b: (chunk, b, half))

    in_specs = [
        spec(2, 0), spec(2, 1), spec(3, 0), spec(3, 1),
        pl.BlockSpec((None, t, LANE), lambda b: (1, b, 3)),
        pl.BlockSpec((None, LANE, 2 * HW), lambda b: (l, 0, 0)),
        pl.BlockSpec((None, 1, 2 * HW), lambda b: (l, 0, 0)),
        pl.BlockSpec((None, 1, GLA_DV), lambda b: (l, 0, 0)),
    ]
    args = [proj, proj, proj, proj, proj, glr_w, glr_b, gla_g]
    if state is not None:
        in_specs.append(pl.BlockSpec((None, None, 2, H_GLA, GLA_DK, GLA_DV), lambda b: (b, l, 0, 0, 0, 0)))
        args.append(state)
    out_specs = [pl.BlockSpec((t, HW), lambda b: (b, 0))]
    out_shape = [jax.ShapeDtypeStruct((R, HW), BF16)]
    if want_state:
        out_specs.append(pl.BlockSpec((seqs, 2, H_GLA, GLA_DK, GLA_DV), lambda b: (b, 0, 0, 0, 0)))
        out_shape.append(jax.ShapeDtypeStruct((nseq, 2, H_GLA, GLA_DK, GLA_DV), F32))
    return pl.pallas_call(
        functools.partial(_gla_kernel, t=t, seqs=seqs, has_state=state is not None, want_state=want_state),
        grid=(nseq // seqs,),
        in_specs=in_specs,
        out_specs=out_specs,
        out_shape=out_shape,
        scratch_shapes=[pltpu.VMEM((t, 2 * HW), F32), pltpu.VMEM((t, HW), F32), pltpu.VMEM((t, HW), F32),
                        pltpu.VMEM((seqs * 2 * H_GLA, GLA_DV, GLA_DK), F32)],
        compiler_params=_cparams(1),
        name="gla",
    )(*args)


def _fft_kernel(u_ref, cw_ref, sw_ref, ct_ref, o_ref, *, t):
    u = u_ref[...]
    a = _dot(u, cw_ref[...]).astype(BF16)
    b = _dot(u, sw_ref[...]).astype(BF16)
    for sq in range(u.shape[0] // t):
        rows = slice(sq * t, (sq + 1) * t)
        ab = jnp.concatenate([a[rows], b[rows]], axis=0)
        o_ref[rows, :] = _dot(ct_ref[...], ab).astype(BF16)


def _fft(proj, cw, sw, ct, *, t):
    R = proj.shape[1]
    W = FFT_WIDTH
    rows = t * _seqs_per_step(t)
    return pl.pallas_call(
        functools.partial(_fft_kernel, t=t),
        grid=(R // rows,),
        in_specs=[
            pl.BlockSpec((None, rows, W), lambda b: (1, b, 1)),
            pl.BlockSpec((W, W), lambda b: (0, 0)),
            pl.BlockSpec((W, W), lambda b: (0, 0)),
            pl.BlockSpec((t, 2 * t), lambda b: (0, 0)),
        ],
        out_specs=pl.BlockSpec((rows, W), lambda b: (b, 0)),
        out_shape=jax.ShapeDtypeStruct((R, W), BF16),
        compiler_params=_cparams(1),
        name="fourier_mix",
    )(proj, cw, sw, ct)


def _mix_kernel(x_ref, om_ref, og_ref, of_ref, g0_ref, g1_ref, g2_ref, mod_ref, n2_ref,
                wm_ref, wg_ref, wf_ref, wo_ref, rw_ref, x1_ref, h2_ref, aff_ref, *, seqs_per_tile):
    ym = _dot(om_ref[...], wm_ref[...])
    yg = _dot(og_ref[...], wg_ref[...])
    yf = _dot(of_ref[...], wf_ref[...])
    merged = g0_ref[...].astype(F32) * ym + g1_ref[...].astype(F32) * yg + g2_ref[...].astype(F32) * yf
    z = _dot(merged.astype(BF16), wo_ref[...])
    x1 = x_ref[...] + mod_ref[2] * z
    x1_ref[...] = x1
    h2 = _rms(x1, n2_ref[...]) * (1.0 + mod_ref[4]) + mod_ref[3]
    h_hi = h2.astype(BF16)
    h2_ref[...] = h_hi
    h_lo = (h2 - h_hi.astype(F32)).astype(BF16)
    lg = _dot(h_hi, rw_ref[...])
    lg = lg[:, :LANE] + lg[:, LANE:] + _dot(h_lo, rw_ref[:, :LANE])
    logits = lg.T[:N_EXPERTS]
    m = jnp.max(logits, axis=0, keepdims=True)
    p = jnp.exp(logits - m)
    aff = p / jnp.sum(p, axis=0, keepdims=True)
    if seqs_per_tile == 0:
        aff_ref[...] = aff
    else:
        t = ROW_TILE // seqs_per_tile
        for s in range(seqs_per_tile):
            aff_ref[s] = aff[:, s * t:(s + 1) * t]


def _mix(x, o_mla, o_gla, o_fft, proj, mods, n2, wm, wg, wf, wo, rwt, l, *, t, mod_row0, rows_per_mod):
    R, D = x.shape
    HW = H_MLA * D_V
    nseq = R // t

    def wspec(k):
        return pl.BlockSpec((None, k, D), lambda i: (l, 0, 0))

    if t <= ROW_TILE:
        seqs_per_tile = ROW_TILE // t
        aff_spec = pl.BlockSpec((seqs_per_tile, N_EXPERTS, t), lambda i: (i, 0, 0))
    else:
        seqs_per_tile = 0
        tiles = t // ROW_TILE
        aff_spec = pl.BlockSpec((None, N_EXPERTS, ROW_TILE), lambda i: (i // tiles, 0, i % tiles))
    return pl.pallas_call(
        functools.partial(_mix_kernel, seqs_per_tile=seqs_per_tile),
        grid=(R // ROW_TILE,),
        in_specs=[
            pl.BlockSpec((ROW_TILE, D), lambda i: (i, 0)),
            pl.BlockSpec((ROW_TILE, HW), lambda i: (i, 0)),
            pl.BlockSpec((ROW_TILE, HW), lambda i: (i, 0)),
            pl.BlockSpec((ROW_TILE, HW), lambda i: (i, 0)),
            pl.BlockSpec((None, ROW_TILE, D), lambda i: (4, i, 0)),
            pl.BlockSpec((None, ROW_TILE, D), lambda i: (5, i, 0)),
            pl.BlockSpec((None, ROW_TILE, D), lambda i: (6, i, 0)),
            _mod_spec(l, mod_row0, rows_per_mod),
            pl.BlockSpec((None, 1, D), lambda i: (l, 0, 0)),
            wspec(HW), wspec(HW), wspec(HW), wspec(D),
            pl.BlockSpec((None, D, 2 * LANE), lambda i: (l, 0, 0)),
        ],
        out_specs=[
            pl.BlockSpec((ROW_TILE, D), lambda i: (i, 0)),
            pl.BlockSpec((ROW_TILE, D), lambda i: (i, 0)),
            aff_spec,
        ],
        out_shape=[
            jax.ShapeDtypeStruct((R, D), F32),
            jax.ShapeDtypeStruct((R, D), BF16),
            jax.ShapeDtypeStruct((nseq, N_EXPERTS, t), F32),
        ],
        compiler_params=_cparams(1),
        name="mix_out",
    )(x, o_mla, o_gla, o_fft, proj, proj, proj, mods, n2, wm, wg, wf, wo, rwt)


def _route_kernel(a_ref, slot_ref, *, cap):
    a = a_ref[...]
    rows, t = a.shape

    def count_ge(x):
        return jnp.sum(jnp.where(a >= x, 1.0, 0.0), axis=1, keepdims=True)

    thr = jnp.zeros((rows, 1), jnp.int32)
    for bit in range(30, -1, -1):
        cand = thr | (1 << bit)
        thr = jnp.where(count_ge(pltpu.bitcast(cand, F32)) >= cap, cand, thr)
    lo = pltpu.bitcast(thr, F32)
    hi = pltpu.bitcast(thr + 1, F32)
    for _ in range(ROUTE_REFINE_STEPS):
        mid = lo + (hi - lo) * 0.5
        keep = count_ge(mid) >= cap
        lo = jnp.where(keep, mid, lo)
        hi = jnp.where(keep, hi, mid)
    gt = a > lo
    eq = a == lo
    need = cap - jnp.sum(jnp.where(gt, 1.0, 0.0), axis=1, keepdims=True)
    ri = lax.broadcasted_iota(jnp.int32, (t, t), 0)
    ci = lax.broadcasted_iota(jnp.int32, (t, t), 1)
    before = jnp.where(ri < ci, 1.0, 0.0).astype(BF16)
    eq_rank = _dot(jnp.where(eq, 1.0, 0.0).astype(BF16), before)
    sel = gt | (eq & (eq_rank < need))
    slot = _dot(jnp.where(sel, 1.0, 0.0).astype(BF16), before)
    slot_ref[...] = jnp.where(sel, slot, -1.0)


def _route_pair_kernel(a1_ref, a2_ref, s1_ref, s2_ref, *, caps):
    _route_kernel(a1_ref, s1_ref, cap=caps[0])
    _route_kernel(a2_ref, s2_ref, cap=caps[1])


def _route_pair(a1, cap1, a2, cap2):
    return pl.pallas_call(
        functools.partial(_route_pair_kernel, caps=(cap1, cap2)),
        out_shape=[jax.ShapeDtypeStruct(a1.shape, F32), jax.ShapeDtypeStruct(a2.shape, F32)],
        compiler_params=pltpu.CompilerParams(vmem_limit_bytes=VMEM_LIMIT),
        name="route",
    )(a1, a2)


def _gather_kernel(h_ref, slot_ref, aff_ref, xg_ref, g_ref, p_ref, *, cap):
    seqs, _, t = slot_ref.shape
    ci = lax.broadcasted_iota(jnp.int32, (cap, t), 0).astype(F32)
    for sq in range(seqs):
        slots = slice(sq * cap, (sq + 1) * cap)
        for e in range(N_EXPERTS):
            hit = ci == slot_ref[sq, e:e + 1, :]
            p_ref[sq, e * cap:(e + 1) * cap, :] = jnp.where(hit, 1.0, 0.0).astype(BF16)
            g_ref[e, slots, :] = jnp.sum(jnp.where(hit, aff_ref[sq, e:e + 1, :], 0.0), axis=1, keepdims=True)
        xg = _dot(p_ref[sq], h_ref[sq * t:(sq + 1) * t, :])
        for e in range(N_EXPERTS):
            xg_ref[e, slots, :] = xg[e * cap:(e + 1) * cap].astype(BF16)


def _gather(h2, slot, aff, *, t, cap):
    R, D = h2.shape
    nseq = R // t
    seqs = _seqs_per_step(t)
    return pl.pallas_call(
        functools.partial(_gather_kernel, cap=cap),
        grid=(nseq // seqs,),
        in_specs=[
            pl.BlockSpec((seqs * t, D), lambda b: (b, 0)),
            pl.BlockSpec((seqs, N_EXPERTS, t), lambda b: (b, 0, 0)),
            pl.BlockSpec((seqs, N_EXPERTS, t), lambda b: (b, 0, 0)),
        ],
        out_specs=[
            pl.BlockSpec((N_EXPERTS, seqs * cap, D), lambda b: (0, b, 0)),
            pl.BlockSpec((N_EXPERTS, seqs * cap, 1), lambda b: (0, b, 0)),
        ],
        out_shape=[
            jax.ShapeDtypeStruct((N_EXPERTS, nseq * cap, D), BF16),
            jax.ShapeDtypeStruct((N_EXPERTS, nseq * cap, 1), F32),
        ],
        scratch_shapes=[pltpu.VMEM((seqs, N_EXPERTS * cap, t), BF16)],
        compiler_params=_cparams(1),
        name="moe_gather",
    )(h2, slot, aff)


def _expert_kernel(xa_ref, xb_ref, ga_ref, gb_ref, w1_ref, w3_ref, w2_ref, ya_ref, yb_ref, acc_ref):
    f = pl.program_id(1)
    nf = pl.num_programs(1)
    na = xa_ref.shape[0]
    groups = ((xa_ref, ga_ref, ya_ref, 0), (xb_ref, gb_ref, yb_ref, na))

    def body(first, last):
        w1 = w1_ref[...].astype(BF16)
        w3 = w3_ref[...].astype(BF16)
        w2 = w2_ref[...].astype(BF16)
        for x_ref, g_ref, y_ref, base in groups:
            n = x_ref.shape[0]
            sub = min(n, EXPERT_ROW_TILE)
            for r in range(n // sub):
                rows = slice(r * sub, (r + 1) * sub)
                arows = slice(base + r * sub, base + (r + 1) * sub)
                x = x_ref[rows, :]
                a = _dot(x, w1)
                gt = _dot(x, w3)
                hm = (a * _sigmoid(a) * gt).astype(BF16)
                contrib = _dot(hm, w2)
                if not first:
                    contrib = acc_ref[arows, :] + contrib
                if last:
                    y_ref[rows, :] = (contrib * g_ref[rows, :]).astype(BF16)
                else:
                    acc_ref[arows, :] = contrib

    @pl.when(f == 0)
    def _():
        body(True, False)

    @pl.when((f > 0) & (f < nf - 1))
    def _():
        body(False, False)

    @pl.when(f == nf - 1)
    def _():
        body(False, True)


def _experts(xa, xb, ga, gb, w1, w3, w2, l):
    E, na, D = xa.shape
    nb = xb.shape[1]
    FT = EXPERT_F_TILE
    assert D_EXPERT // FT >= 2

    def rows_spec(n, w):
        return pl.BlockSpec((None, n, w), lambda e, f: (e, 0, 0))

    return pl.pallas_call(
        _expert_kernel,
        grid=(E, D_EXPERT // FT),
        in_specs=[
            rows_spec(na, D), rows_spec(nb, D), rows_spec(na, 1), rows_spec(nb, 1),
            pl.BlockSpec((None, None, D, FT), lambda e, f: (l, e, 0, f)),
            pl.BlockSpec((None, None, D, FT), lambda e, f: (l, e, 0, f)),
            pl.BlockSpec((None, None, FT, D), lambda e, f: (l, e, f, 0)),
        ],
        out_specs=[rows_spec(na, D), rows_spec(nb, D)],
        out_shape=[jax.ShapeDtypeStruct((E, na, D), BF16), jax.ShapeDtypeStruct((E, nb, D), BF16)],
        scratch_shapes=[pltpu.VMEM((na + nb, D), F32)],
        compiler_params=_cparams(2),
        name="moe_experts",
    )(xa, xb, ga, gb, w1, w3, w2)


def _scatter_kernel(x_ref, y_ref, slot_ref, mod_ref, *rest, cap):
    o_ref = rest[-1]
    seqs, _, t = slot_ref.shape
    n = N_EXPERTS * cap
    er = lax.broadcasted_iota(jnp.int32, (N_EXPERTS, n), 0)
    ec = lax.broadcasted_iota(jnp.int32, (N_EXPERTS, n), 1)
    expand = jnp.where(ec // cap == er, 1.0, 0.0).astype(BF16)
    want = (lax.broadcasted_iota(jnp.int32, (t, n), 1) % cap).astype(F32)
    for sq in range(seqs):
        rows = slice(sq * t, (sq + 1) * t)
        slot_wide = _dot_tn(slot_ref[sq].astype(BF16), expand)
        pt = jnp.where(slot_wide == want, 1.0, 0.0).astype(BF16)
        y = jnp.concatenate([y_ref[e, sq * cap:(sq + 1) * cap, :] for e in range(N_EXPERTS)], axis=0)
        x2 = x_ref[rows, :] + mod_ref[5] * _dot(pt, y)
        o_ref[rows, :] = x2 if len(rest) == 1 else _rms(x2, rest[0][...])


def _scatter(x1, y, slot, mods, l, *, t, cap, mod_row0, rows_per_mod, final_g=None):
    R, D = x1.shape
    seqs = _seqs_per_step(t)
    assert seqs == 1 or rows_per_mod is None
    in_specs = [
        pl.BlockSpec((seqs * t, D), lambda b: (b, 0)),
        pl.BlockSpec((N_EXPERTS, seqs * cap, D), lambda b: (0, b, 0)),
        pl.BlockSpec((seqs, N_EXPERTS, t), lambda b: (b, 0, 0)),
        _mod_spec(l, mod_row0, rows_per_mod),
    ]
    args = [x1, y, slot, mods]
    if final_g is not None:
        in_specs.append(pl.BlockSpec((1, D), lambda b: (0, 0)))
        args.append(final_g)
    return pl.pallas_call(
        functools.partial(_scatter_kernel, cap=cap),
        grid=(R // (seqs * t),),
        in_specs=in_specs,
        out_specs=pl.BlockSpec((seqs * t, D), lambda b: (b, 0)),
        out_shape=jax.ShapeDtypeStruct((R, D), F32),
        compiler_params=_cparams(1),
        name="moe_scatter",
    )(*args)


def _stack_kernel(*refs):
    ins, (ckv_ref, kr_ref, st_ref) = refs[:-3], refs[-3:]
    n = len(ins) // 3
    for l in range(n):
        ckv_ref[:, l] = ins[l][...]
        kr_ref[:, l] = ins[n + l][:, :, KR_OFF:KR_OFF + D_ROPE]
        st_ref[:, l] = ins[2 * n + l][...]


def _stack_layers(ckv_l, krp_l, st_l):
    n = len(ckv_l)
    B, T, _ = ckv_l[0].shape
    sb = STACK_SEQS

    def spec(shape):
        zeros = (0,) * (len(shape) - 1)
        return pl.BlockSpec((sb,) + tuple(shape[1:]), lambda b: (b,) + zeros)

    def out_spec(shape):
        zeros = (0,) * len(shape[1:])
        return pl.BlockSpec((sb, n) + tuple(shape[1:]), lambda b: (b, 0) + zeros)

    st_shape = st_l[0].shape
    return pl.pallas_call(
        _stack_kernel,
        grid=(B // sb,),
        in_specs=[spec(a.shape) for a in (*ckv_l, *krp_l, *st_l)],
        out_specs=[out_spec((B, T, KV_RANK)), out_spec((B, T, D_ROPE)), out_spec(st_shape)],
        out_shape=[
            jax.ShapeDtypeStruct((B, n, T, KV_RANK), F32),
            jax.ShapeDtypeStruct((B, n, T, D_ROPE), F32),
            jax.ShapeDtypeStruct((B, n) + tuple(st_shape[1:]), F32),
        ],
        compiler_params=_cparams(1),
        name="stack_layers",
    )(*ckv_l, *krp_l, *st_l)


def _rope_tables(t_lat):
    half = D_ROPE // 2
    nfreq = half // 2
    pos = np.arange(t_lat)
    inv = 1.0 / (ROPE_BASE ** (np.arange(0, half, 2, dtype=np.float64) / half))
    cos = np.ones((t_lat, HEAD_PAD))
    s_up = np.zeros((t_lat, HEAD_PAD))
    s_dn = np.zeros((t_lat, HEAD_PAD))
    for part, p in enumerate((pos // GRID_W, pos % GRID_W)):
        ang = p[:, None].astype(np.float64) * inv
        o = KR_OFF + part * half
        cos[:, o:o + nfreq] = np.cos(ang)
        cos[:, o + nfreq:o + half] = np.cos(ang)
        s_up[:, o:o + nfreq] = -np.sin(ang)
        s_dn[:, o + nfreq:o + half] = np.sin(ang)
    tab = np.stack([cos, s_up, s_dn], axis=0).reshape(3, t_lat // ROW_TILE, ROW_TILE, HEAD_PAD)
    return jnp.asarray(np.moveaxis(tab, 1, 0), dtype=F32)


def _dft_mats(n, scale):
    k = np.arange(n)
    ang = (2.0 * np.pi / n) * ((k[:, None] * k[None, :]) % n)
    return np.cos(ang) * scale, np.sin(ang) * scale


def _dft_tables(t):
    cw, sw = _dft_mats(FFT_GROUP_W, FFT_GROUP_W ** -0.5)
    eye = np.eye(FFT_GROUPS)
    ct, st = _dft_mats(t, t ** -0.5)

    def const(a):
        return jnp.asarray(a, dtype=F32).astype(BF16)

    return const(np.kron(eye, cw)), const(np.kron(eye, sw)), const(np.concatenate([ct, -st], axis=1))


def _layout_w_in(w_in):
    L, D, _ = w_in.shape
    sizes = (H_MLA * (D_NOPE + D_ROPE), KV_RANK, D_ROPE, H_GLA * GLA_DK, H_GLA * GLA_DK, H_GLA * GLA_DV,
             H_GLA * GLA_DV, 2 * GLA_GATE_RANK, FFT_WIDTH, 3 * D_MODEL)
    offs = np.concatenate([[0], np.cumsum(sizes)])
    seg = [w_in[:, :, offs[i]:offs[i + 1]] for i in range(len(sizes))]
    q, ckv, kr, gq, gk, gv, gr, glr, uf, mg = seg
    q = q.reshape(L, D, H_MLA, D_NOPE + D_ROPE)
    q = jnp.pad(q, ((0, 0), (0, 0), (0, 0), (0, HEAD_PAD - D_NOPE - D_ROPE))).reshape(L, D, H_MLA * HEAD_PAD)
    kr = jnp.pad(kr, ((0, 0), (0, 0), (KR_OFF, HEAD_PAD - KR_OFF - D_ROPE)))
    glr = jnp.pad(glr, ((0, 0), (0, 0), (0, LANE - 2 * GLA_GATE_RANK)))
    cols = jnp.concatenate([q, ckv, kr, glr, uf, gq, gk, gv, gr, mg], axis=-1).astype(BF16)
    return jnp.moveaxis(cols.reshape(L, D, N_CHUNK, D), 2, 1)


def _layout_w_ukv(w_ukv):
    L = w_ukv.shape[0]
    w = w_ukv.reshape(L, KV_RANK, H_MLA, D_NOPE + D_V)
    k = jnp.pad(w[..., :D_NOPE], ((0, 0), (0, 0), (0, 0), (0, HEAD_PAD - D_NOPE)))
    v = w[..., D_NOPE:].reshape(L, KV_RANK, H_MLA // 2, 2, D_V)
    z = jnp.zeros_like(v[:, :, :, 0])
    v_even = jnp.concatenate([v[:, :, :, 0], z], axis=-1)
    v_odd = jnp.concatenate([z, v[:, :, :, 1]], axis=-1)
    v2 = jnp.stack([v_even, v_odd], axis=3)
    return (k.reshape(L, KV_RANK, H_MLA * HEAD_PAD).astype(BF16),
            v2.reshape(L, KV_RANK, H_MLA * HEAD_PAD).astype(BF16))


def _layout_gate(gate_w, gate_b):
    L = gate_w.shape[0]
    HW = H_GLA * GLA_DK
    wg = jnp.zeros((L, LANE, 2 * HW), F32)
    for d in range(2):
        wg = wg.at[:, d * GLA_GATE_RANK:(d + 1) * GLA_GATE_RANK, d * HW:(d + 1) * HW].set(gate_w[:, d])
    return wg.astype(BF16), gate_b.reshape(L, 1, 2 * HW)


def kernel(x_prompt, x_sample, cache_ckv, cache_krope, state_gla, c, c_ctx, ada_w, ada_b, norm1_g, norm2_g, w_in, mla_kv_norm_g, mla_w_ukv, gla_gate_w, gla_gate_b, gla_norm_g, w_mla_out, w_gla_out, w_fft_out, w_o, router_w, exp_w1, exp_w3, exp_w2, final_norm_g):
    nb_c, t_c, D = x_prompt.shape
    nb_l, t_l, _ = x_sample.shape
    assert (nb_c * t_c) % ROW_TILE == 0 and ROW_TILE % t_c == 0 and t_l % ROW_TILE == 0 and 1 + nb_l <= MOD_ROWS
    cap_c = CAPACITY_FACTOR * t_c // N_EXPERTS
    cap_l = CAPACITY_FACTOR * t_l // N_EXPERTS

    c_all = jnp.zeros((MOD_ROWS, D), F32).at[0].set(c_ctx).at[1:1 + nb_l].set(c)
    mods = _mods(c_all, ada_w, ada_b).reshape(DEPTH, N_MOD, MOD_ROWS, 1, D)
    rope_tab = _rope_tables(t_l)
    dft_c = _dft_tables(t_c)
    dft_l = _dft_tables(t_l)
    w_in_r = _layout_w_in(w_in)
    wuk, wuv = _layout_w_ukv(mla_w_ukv)
    wgate, bgate = _layout_gate(gla_gate_w, gla_gate_b)
    n1 = norm1_g.reshape(DEPTH, 1, D)
    n2 = norm2_g.reshape(DEPTH, 1, D)
    kvg = mla_kv_norm_g.reshape(DEPTH, 1, KV_RANK)
    glg = gla_norm_g.reshape(DEPTH, 1, GLA_DV)
    wm = w_mla_out.astype(BF16)
    wgo = w_gla_out.astype(BF16)
    wfo = w_fft_out.astype(BF16)
    wo = w_o.astype(BF16)
    rw_hi = router_w.astype(BF16)
    rw_lo = (router_w - rw_hi.astype(F32)).astype(BF16)
    rwt = jnp.zeros((DEPTH, D, 2 * LANE), BF16).at[:, :, :N_EXPERTS].set(rw_hi)
    rwt = rwt.at[:, :, LANE:LANE + N_EXPERTS].set(rw_lo)
    ckr_pad = jnp.pad(cache_krope, ((0, 0), (0, 0), (0, 0), (KR_OFF, HEAD_PAD - KR_OFF - D_ROPE)))

    ctx = dict(t=t_c, cap=cap_c, mod_row0=0, rows_per_mod=None)
    lat = dict(t=t_l, cap=cap_l, mod_row0=1, rows_per_mod=t_l // ROW_TILE)
    xs = [x_prompt.reshape(nb_c * t_c, D), x_sample.reshape(nb_l * t_l, D)]
    ckv_out, kr_out, st_out = [], [], []

    def pre_experts(x, grp, l, is_ctx):
        t, cap = grp["t"], grp["cap"]
        mod_kw = dict(mod_row0=grp["mod_row0"], rows_per_mod=grp["rows_per_mod"])
        proj, ckv, krp = _inproj(x, mods, n1, kvg, w_in_r, l, rope_tab=None if is_ctx else rope_tab, **mod_kw)
        if is_ctx:
            ckv_out.append(ckv.reshape(nb_c, t_c, KV_RANK))
            kr_out.append(krp.reshape(nb_c, t_c, HEAD_PAD))
            o_mla = _attn(proj, ckv, krp, wuk, wuv, l, t_q=t)
            o_gla, st = _gla(proj, wgate, bgate, glg, l, t=t, want_state=True)
            st_out.append(st)
            o_fft = _fft(proj, *dft_c, t=t)
        else:
            o_mla = _attn(proj, ckv, krp, wuk, wuv, l, t_q=t, cache=(cache_ckv, ckr_pad))
            (o_gla,) = _gla(proj, wgate, bgate, glg, l, t=t, state=state_gla)
            o_fft = _fft(proj, *dft_l, t=t)
        rows_per_mod_mix = grp["rows_per_mod"]
        x1, h2, aff = _mix(x, o_mla, o_gla, o_fft, proj, mods, n2, wm, wgo, wfo, wo, rwt, l, t=t,
                           mod_row0=grp["mod_row0"], rows_per_mod=rows_per_mod_mix)
        return x1, h2, aff

    for l in range(DEPTH):
        x1_c, h2_c, aff_c = pre_experts(xs[0], ctx, l, True)
        x1_l, h2_l, aff_l = pre_experts(xs[1], lat, l, False)
        slot_c, slot_l = _route_pair(aff_c.reshape(nb_c * N_EXPERTS, t_c), cap_c,
                                     aff_l.reshape(nb_l * N_EXPERTS, t_l), cap_l)
        slot_c = slot_c.reshape(nb_c, N_EXPERTS, t_c)
        slot_l = slot_l.reshape(nb_l, N_EXPERTS, t_l)
        xg_c, g_c = _gather(h2_c, slot_c, aff_c, t=t_c, cap=cap_c)
        xg_l, g_l = _gather(h2_l, slot_l, aff_l, t=t_l, cap=cap_l)
        y_c, y_l = _experts(xg_c, xg_l, g_c, g_l, exp_w1, exp_w3, exp_w2, l)
        fg = final_norm_g.reshape(1, D) if l == DEPTH - 1 else None
        xs = [
            _scatter(x1_c, y_c, slot_c, mods, l, t=t_c, cap=cap_c, mod_row0=0, rows_per_mod=None, final_g=fg),
            _scatter(x1_l, y_l, slot_l, mods, l, t=t_l, cap=cap_l, mod_row0=1, rows_per_mod=1, final_g=fg),
        ]

    y_prompt = xs[0].reshape(nb_c, t_c, D)
    y_sample = xs[1].reshape(nb_l, t_l, D)
    return (y_prompt, y_sample) + tuple(_stack_layers(ckv_out, kr_out, st_out))
```
